```python
import math, functools
import jax, jax.numpy as jnp
from jax import lax
import numpy as np

D_MODEL = 1024
BATCH = 4
SEQ = 4096
DEPTH = 2
DEC_BATCH = 32
DEC_SEQ = 8
PAST_LEN = 16384
PAGE_SIZE = 128

N_HEADS = 8
QK_DIM = D_MODEL // (2 * N_HEADS)
K_ROW = 2 * QK_DIM
V_DIM = 2 * QK_DIM
ROPE_DIM = QK_DIM // 4
ROPE_THETA = 500000.0
CONV_WIDTH = D_MODEL
CONV_K = 31
X_HEADS = 4
X_DIM = D_MODEL // X_HEADS
N_MEM = 256
N_SPLIT = 12
Q_BLOCK = 128
RMS_EPS = 1e-6
LN_EPS = 1e-5
SUBLN_EPS = 1e-5

kernel_name = 'hybrid_conv_diffattn_memxattn_decoder_step'


def _rmsnorm(x, g, eps=RMS_EPS):
    xf = x.astype(jnp.float32)
    y = xf * lax.rsqrt(jnp.mean(xf * xf, axis=-1, keepdims=True) + eps)
    return (y * g.astype(jnp.float32)).astype(x.dtype)


def _layernorm(x, g, b):
    xf = x.astype(jnp.float32)
    mu = jnp.mean(xf, axis=-1, keepdims=True)
    xc = xf - mu
    var = jnp.mean(xc * xc, axis=-1, keepdims=True)
    y = xc * lax.rsqrt(var + LN_EPS) * g.astype(jnp.float32) + b.astype(jnp.float32)
    return y.astype(x.dtype)


def _rope(x, pos):
    half = ROPE_DIM // 2
    inv = ROPE_THETA ** (-jnp.arange(half, dtype=jnp.float32) * 2.0 / ROPE_DIM)
    ang = pos.astype(jnp.float32)[:, None] * inv[None, :]
    cos = jnp.cos(ang)[None, :, None, None, :]
    sin = jnp.sin(ang)[None, :, None, None, :]
    xf = x.astype(jnp.float32)
    x1 = xf[..., :half]
    x2 = xf[..., half:ROPE_DIM]
    out = jnp.concatenate([x1 * cos - x2 * sin, x2 * cos + x1 * sin, xf[..., ROPE_DIM:]], axis=-1)
    return out.astype(x.dtype)


def _diff_lambda(lq1, lk1, lq2, lk2, lam_init):
    f = lambda t: t.astype(jnp.float32)
    return jnp.exp(jnp.sum(f(lq1) * f(lk1))) - jnp.exp(jnp.sum(f(lq2) * f(lk2))) + lam_init


def _diff_attn_prompt(q, k, v, lam):
    B, S = q.shape[0], q.shape[1]
    nb = S // Q_BLOCK
    scale = QK_DIM ** -0.5
    kpos = jnp.arange(S)
    vf = v.astype(jnp.float32)
    qb = jnp.moveaxis(q.reshape(B, nb, Q_BLOCK, N_HEADS, 2, QK_DIM), 1, 0)

    def block(args):
        qi, i = args
        s = jnp.einsum('bqhmd,bkhmd->bhmqk', qi, k).astype(jnp.float32) * scale
        qpos = i * Q_BLOCK + jnp.arange(Q_BLOCK)
        mask = kpos[None, :] <= qpos[:, None]
        a = jax.nn.softmax(jnp.where(mask, s, -jnp.inf), axis=-1)
        p = a[:, :, 0] - lam * a[:, :, 1]
        return jnp.einsum('bhqk,bkhe->bqhe', p, vf)

    o = lax.map(block, (qb, jnp.arange(nb)))
    return jnp.moveaxis(o, 0, 1).reshape(B, S, N_HEADS, V_DIM).astype(v.dtype)


def _online_update(carry, s, v):
    m, l, acc = carry
    m_new = jnp.maximum(m, jnp.max(s, axis=-1))
    c = jnp.exp(m - m_new)
    p = jnp.exp(s - m_new[..., None])
    acc = acc * c[..., None] + jnp.einsum('bhmqk,bkhe->bhmqe', p, v.astype(jnp.float32))
    return (m_new, l * c + jnp.sum(p, axis=-1), acc)


def _diff_attn_sample(q, k, v, lam, cache_k, cache_v, page_table, layer):
    B, T = q.shape[0], q.shape[1]
    scale = QK_DIM ** -0.5
    init = (jnp.full((B, N_HEADS, 2, T), -jnp.inf, jnp.float32),
            jnp.zeros((B, N_HEADS, 2, T), jnp.float32),
            jnp.zeros((B, N_HEADS, 2, T, V_DIM), jnp.float32))

    def step(carry, phys):
        kp = cache_k[layer, phys].reshape(B, PAGE_SIZE, N_HEADS, 2, QK_DIM)
        vp = cache_v[layer, phys]
        s = jnp.einsum('bqhmd,bkhmd->bhmqk', q, kp).astype(jnp.float32) * scale
        return _online_update(carry, s, vp), None

    carry, _ = lax.scan(step, init, page_table.T)
    s = jnp.einsum('bqhmd,bkhmd->bhmqk', q, k).astype(jnp.float32) * scale
    mask = jnp.tril(jnp.ones((T, T), dtype=bool))
    m, l, acc = _online_update(carry, jnp.where(mask, s, -jnp.inf), v)
    o = acc / l[..., None]
    o = o[:, :, 0] - lam * o[:, :, 1]
    return jnp.transpose(o, (0, 2, 1, 3)).astype(v.dtype)


def _conv_branch(a, b, z, prev, w_dw, b_dw, ln_g, ln_b, w_pc):
    u = a * jax.nn.sigmoid(b)
    ext = jnp.concatenate([prev.astype(u.dtype), u], axis=1)
    c = lax.conv_general_dilated(ext, w_dw[:, None, :].astype(u.dtype), window_strides=(1,),
                                 padding='VALID', dimension_numbers=('NWC', 'WIO', 'NWC'),
                                 feature_group_count=u.shape[-1]) + b_dw
    y = jax.nn.silu(_layernorm(c, ln_g, ln_b)) * jax.nn.silu(z)
    return y @ w_pc, ext[:, ext.shape[1] - (CONV_K - 1):]


def _diff_out(o, subln_g, lam_init, z, w_pa):
    B, T = o.shape[0], o.shape[1]
    on = _rmsnorm(o, subln_g, SUBLN_EPS) * (1.0 - lam_init)
    return (on.reshape(B, T, N_HEADS * V_DIM) * jax.nn.silu(z)) @ w_pa


def _mem_kv(mem, g, w_mk, w_mv):
    B = mem.shape[0]
    mn = _rmsnorm(mem, g)
    return ((mn @ w_mk).reshape(B, N_MEM, X_HEADS, X_DIM),
            (mn @ w_mv).reshape(B, N_MEM, X_HEADS, X_DIM))


def _cross(q, z, mk, mv, w_px):
    B, T = q.shape[0], q.shape[1]
    qh = q.reshape(B, T, X_HEADS, X_DIM)
    s = jnp.einsum('bqhd,bkhd->bhqk', qh, mk).astype(jnp.float32) * (X_DIM ** -0.5)
    a = jax.nn.softmax(s, axis=-1)
    o = jnp.einsum('bhqk,bkhd->bqhd', a, mv.astype(jnp.float32)).astype(q.dtype)
    return (o.reshape(B, T, D_MODEL) * jax.nn.silu(z)) @ w_px


def _layer(x, pos, conv_prev, attend, mem_k, mem_v, norm_g, w_in, w_dw, b_dw, ln_g, ln_b,
           w_pc, subln_g, lam_init, w_pa, w_px, w_o):
    B, T = x.shape[0], x.shape[1]
    a, b, zc, q, k, v, za, qx, zx, gc, ga, gx = jnp.split(_rmsnorm(x, norm_g) @ w_in, N_SPLIT, axis=-1)
    y_c, conv_new = _conv_branch(a, b, zc, conv_prev, w_dw, b_dw, ln_g, ln_b, w_pc)
    qh = _rope(q.reshape(B, T, N_HEADS, 2, QK_DIM), pos)
    kh = _rope(k.reshape(B, T, N_HEADS, 2, QK_DIM), pos)
    vh = v.reshape(B, T, N_HEADS, V_DIM)
    y_a = _diff_out(attend(qh, kh, vh), subln_g, lam_init, za, w_pa)
    y_x = _cross(qx, zx, mem_k, mem_v, w_px)
    merged = jax.nn.sigmoid(gc) * y_c + jax.nn.sigmoid(ga) * y_a + jax.nn.sigmoid(gx) * y_x
    return x + merged @ w_o, kh.reshape(B, T, N_HEADS, K_ROW), vh, conv_new


def setup_inputs(seed: int = 0) -> dict:
    key = jax.random.key(seed)
    ks = jax.random.split(key, 32)
    n_pages = PAST_LEN // PAGE_SIZE
    n_used = DEC_BATCH * n_pages
    n_phys = n_used + (n_used + 3) // 4
    f32 = jnp.float32

    def nrm(k, shape, scale=1.0):
        return jax.random.normal(k, shape, f32) * scale

    page_table = jax.random.permutation(ks[0], n_phys)[:n_used].reshape(DEC_BATCH, n_pages).astype(jnp.int32)
    return {
        'x_prompt': nrm(ks[1], (BATCH, SEQ, D_MODEL)),
        'x_sample': nrm(ks[2], (DEC_BATCH, DEC_SEQ, D_MODEL)),
        'cache_k': nrm(ks[3], (DEPTH, n_phys, PAGE_SIZE, N_HEADS, K_ROW)),
        'cache_v': nrm(ks[4], (DEPTH, n_phys, PAGE_SIZE, N_HEADS, V_DIM)),
        'cache_conv': nrm(ks[5], (DEPTH, DEC_BATCH, CONV_K - 1, CONV_WIDTH), 0.5),
        'cache_mem_k': nrm(ks[6], (DEPTH, DEC_BATCH, N_MEM, X_HEADS, X_DIM)),
        'cache_mem_v': nrm(ks[7], (DEPTH, DEC_BATCH, N_MEM, X_HEADS, X_DIM)),
        'page_table': page_table,
        'mem_prompt': nrm(ks[8], (BATCH, N_MEM, D_MODEL)),
        'norm_g': 1.0 + nrm(ks[9], (DEPTH, D_MODEL), 0.02),
        'w_in': nrm(ks[10], (DEPTH, D_MODEL, N_SPLIT * D_MODEL), D_MODEL ** -0.5),
        'w_dw': nrm(ks[11], (DEPTH, CONV_K, CONV_WIDTH), CONV_K ** -0.5),
        'b_dw': nrm(ks[12], (DEPTH, CONV_WIDTH), 0.02),
        'conv_ln_g': 1.0 + nrm(ks[13], (DEPTH, CONV_WIDTH), 0.02),
        'conv_ln_b': nrm(ks[14], (DEPTH, CONV_WIDTH), 0.02),
        'w_pc': nrm(ks[15], (DEPTH, CONV_WIDTH, D_MODEL), CONV_WIDTH ** -0.5),
        'lam_q1': nrm(ks[16], (DEPTH, QK_DIM), 0.1),
        'lam_k1': nrm(ks[17], (DEPTH, QK_DIM), 0.1),
        'lam_q2': nrm(ks[18], (DEPTH, QK_DIM), 0.1),
        'lam_k2': nrm(ks[19], (DEPTH, QK_DIM), 0.1),
        'subln_g': 1.0 + nrm(ks[20], (DEPTH, V_DIM), 0.02),
        'w_pa': nrm(ks[21], (DEPTH, N_HEADS * V_DIM, D_MODEL), (N_HEADS * V_DIM) ** -0.5),
        'mem_norm_g': 1.0 + nrm(ks[22], (DEPTH, D_MODEL), 0.02),
        'w_mk': nrm(ks[23], (DEPTH, D_MODEL, X_HEADS * X_DIM), D_MODEL ** -0.5),
        'w_mv': nrm(ks[24], (DEPTH, D_MODEL, X_HEADS * X_DIM), D_MODEL ** -0.5),
        'w_px': nrm(ks[25], (DEPTH, X_HEADS * X_DIM, D_MODEL), (X_HEADS * X_DIM) ** -0.5),
        'w_o': nrm(ks[26], (DEPTH, D_MODEL, D_MODEL), D_MODEL ** -0.5),
        'final_g': 1.0 + nrm(ks[27], (D_MODEL,), 0.02),
    }


def reference(x_prompt, x_sample, cache_k, cache_v, cache_conv, cache_mem_k, cache_mem_v, page_table,
              mem_prompt, norm_g, w_in, w_dw, b_dw, conv_ln_g, conv_ln_b, w_pc, lam_q1, lam_k1, lam_q2,
              lam_k2, subln_g, w_pa, mem_norm_g, w_mk, w_mv, w_px, w_o, final_g):
    pos_p = jnp.arange(x_prompt.shape[1])
    pos_s = PAST_LEN + jnp.arange(x_sample.shape[1])
    xp, xs = x_prompt, x_sample
    kp_l, vp_l, cp_l, mkp_l, mvp_l, ks_l, vs_l, cs_l = [], [], [], [], [], [], [], []
    for l in range(DEPTH):
        lam_init = 0.8 - 0.6 * math.exp(-0.3 * l)
        lam = _diff_lambda(lam_q1[l], lam_k1[l], lam_q2[l], lam_k2[l], lam_init)
        shared = (norm_g[l], w_in[l], w_dw[l], b_dw[l], conv_ln_g[l], conv_ln_b[l], w_pc[l],
                  subln_g[l], lam_init, w_pa[l], w_px[l], w_o[l])
        mk_p, mv_p = _mem_kv(mem_prompt, mem_norm_g[l], w_mk[l], w_mv[l])
        conv0 = jnp.zeros((xp.shape[0], CONV_K - 1, CONV_WIDTH), xp.dtype)
        xp, k_p, v_p, c_p = _layer(xp, pos_p, conv0, functools.partial(_diff_attn_prompt, lam=lam),
                                   mk_p, mv_p, *shared)
        attend_s = functools.partial(_diff_attn_sample, lam=lam, cache_k=cache_k, cache_v=cache_v,
                                     page_table=page_table, layer=l)
        xs, k_s, v_s, c_s = _layer(xs, pos_s, cache_conv[l], attend_s, cache_mem_k[l], cache_mem_v[l], *shared)
        kp_l.append(k_p); vp_l.append(v_p); cp_l.append(c_p); mkp_l.append(mk_p); mvp_l.append(mv_p)
        ks_l.append(k_s); vs_l.append(v_s); cs_l.append(c_s)
    y_prompt = _rmsnorm(xp, final_g)
    y_sample = _rmsnorm(xs, final_g)
    return (y_prompt, y_sample, jnp.stack(kp_l), jnp.stack(vp_l), jnp.stack(cp_l), jnp.stack(mkp_l),
            jnp.stack(mvp_l), jnp.stack(ks_l), jnp.stack(vs_l), jnp.stack(cs_l))
```

```python
import functools
import math

import jax
import jax.numpy as jnp
from jax import lax
from jax.experimental import pallas as pl
from jax.experimental.pallas import tpu as pltpu

D_MODEL = 1024
N_HEADS = 8
QK_DIM = 64
HEAD_W = 128
ROPE_DIM = 16
ROPE_THETA = 500000.0
CONV_K = 31
X_HEADS = 4
X_DIM = 256
N_MEM = 256
N_SPLIT = 12
PAGE = 128
RMS_EPS = 1e-6
LN_EPS = 1e-5
SUBLN_EPS = 1e-5
LANES = 128
CONV_HALO = 32

VMEM_LIMIT = 52 * 1024 * 1024

SLOT_U, SLOT_SZC, SLOT_Q, SLOT_K, SLOT_V, SLOT_SZA, SLOT_QX, SLOT_SZX, SLOT_SGC, SLOT_SGA, SLOT_SGX = range(11)
N_SLOTS = 11

BF16 = jnp.bfloat16
F32 = jnp.float32


def _params(sem):
    return pltpu.CompilerParams(dimension_semantics=sem, vmem_limit_bytes=VMEM_LIMIT)


def _sigmoid(x):
    return 1.0 / (1.0 + jnp.exp(-x))


def _silu(x):
    return x * _sigmoid(x)


def _rms_rows(x, eps):
    return x * lax.rsqrt(jnp.mean(x * x, axis=-1, keepdims=True) + eps)


def _dot(a, b):
    return jnp.dot(a, b, preferred_element_type=F32)


def _dot_nt(a, b):
    return lax.dot_general(a, b, (((1,), (1,)), ((), ())), preferred_element_type=F32)


def _diff_lambda(lam_ref, lam_init):
    lv = lam_ref[...]
    e1 = jnp.exp(jnp.sum(lv[0:1] * lv[1:2], axis=-1, keepdims=True))
    e2 = jnp.exp(jnp.sum(lv[2:3] * lv[3:4], axis=-1, keepdims=True))
    return e1 - e2 + lam_init


def _rope_cols(acc, cos, sin_lo, sin_hi):
    outs = []
    for c in range(D_MODEL // LANES):
        xc = acc[:, c * LANES:(c + 1) * LANES]
        outs.append(xc * cos + pltpu.roll(xc, LANES - 8, 1) * sin_lo + pltpu.roll(xc, 8, 1) * sin_hi)
    return jnp.concatenate(outs, axis=1)


def _proj_kernel(x_ref, g_ref, w_ref, cos_ref, slo_ref, shi_ref, p_ref, kv_ref, h_scr, a_scr):
    j = pl.program_id(1)

    @pl.when(j == 0)
    def _():
        h_scr[...] = (_rms_rows(x_ref[...], RMS_EPS) * g_ref[...]).astype(BF16)

    acc = _dot(h_scr[...], w_ref[...])
    dt = p_ref.dtype

    @pl.when(j == 0)
    def _():
        a_scr[...] = acc

    @pl.when(j == 1)
    def _():
        p_ref[...] = (a_scr[...] * _sigmoid(acc)).astype(dt)

    @pl.when((j == 2) | (j == 6) | (j == 8))
    def _():
        p_ref[...] = _silu(acc).astype(dt)

    @pl.when(j == 3)
    def _():
        q = _rope_cols(acc, cos_ref[...], slo_ref[...], shi_ref[...])
        p_ref[...] = (q * (QK_DIM ** -0.5)).astype(dt)

    @pl.when(j == 4)
    def _():
        k = _rope_cols(acc, cos_ref[...], slo_ref[...], shi_ref[...])
        kv_ref[...] = k
        p_ref[...] = k.astype(dt)

    @pl.when(j == 5)
    def _():
        kv_ref[...] = acc
        p_ref[...] = acc.astype(dt)

    @pl.when(j == 7)
    def _():
        p_ref[...] = (acc * (X_DIM ** -0.5)).astype(dt)

    @pl.when(j >= 9)
    def _():
        p_ref[...] = _sigmoid(acc).astype(dt)


def _proj(x, g, w, cos, slo, shi, *, tm, table_blocks, out_dtype):
    m = x.shape[0]
    grid = (m // tm, N_SPLIT)
    tab_spec = pl.BlockSpec((tm, LANES), lambda i, j: (i % table_blocks, 0))
    return pl.pallas_call(
        _proj_kernel,
        grid=grid,
        in_specs=[
            pl.BlockSpec((tm, D_MODEL), lambda i, j: (i, 0)),
            pl.BlockSpec((1, D_MODEL), lambda i, j: (0, 0)),
            pl.BlockSpec((D_MODEL, D_MODEL), lambda i, j: (0, j)),
            tab_spec, tab_spec, tab_spec,
        ],
        out_specs=[
            pl.BlockSpec((None, tm, D_MODEL), lambda i, j: (jnp.maximum(j - 1, 0), i, 0)),
            pl.BlockSpec((None, tm, D_MODEL), lambda i, j: (jnp.clip(j - 4, 0, 1), i, 0)),
        ],
        out_shape=[
            jax.ShapeDtypeStruct((N_SLOTS, m, D_MODEL), out_dtype),
            jax.ShapeDtypeStruct((2, m, D_MODEL), F32),
        ],
        scratch_shapes=[pltpu.VMEM((tm, D_MODEL), BF16), pltpu.VMEM((tm, D_MODEL), F32)],
        compiler_params=_params(("parallel", "arbitrary")),
        name="proj",
    )(x, g, w, cos, slo, shi)


def _memkv_kernel(x_ref, g_ref, w_ref, o_ref, h_scr):
    @pl.when(pl.program_id(1) == 0)
    def _():
        h_scr[...] = (_rms_rows(x_ref[...], RMS_EPS) * g_ref[...]).astype(BF16)

    o_ref[...] = _dot(h_scr[...], w_ref[...])


def _memkv(mem, g, w, *, tm):
    m = mem.shape[0]
    n_groups = w.shape[1] // D_MODEL
    return pl.pallas_call(
        _memkv_kernel,
        grid=(m // tm, n_groups),
        in_specs=[
            pl.BlockSpec((tm, D_MODEL), lambda i, j: (i, 0)),
            pl.BlockSpec((1, D_MODEL), lambda i, j: (0, 0)),
            pl.BlockSpec((D_MODEL, D_MODEL), lambda i, j: (0, j)),
        ],
        out_specs=pl.BlockSpec((None, tm, D_MODEL), lambda i, j: (j, i, 0)),
        out_shape=jax.ShapeDtypeStruct((n_groups, m, D_MODEL), F32),
        scratch_shapes=[pltpu.VMEM((tm, D_MODEL), BF16)],
        compiler_params=_params(("parallel", "arbitrary")),
        name="memkv",
    )(mem, g, w)


def _split_maps(q):
    lane = lax.broadcasted_iota(jnp.int32, q.shape, 1)
    zero = jnp.zeros_like(q)
    return jnp.concatenate([jnp.where(lane < QK_DIM, q, zero), jnp.where(lane >= QK_DIM, q, zero)], axis=0)


def _online_step(carry, s, v):
    m, l, acc = carry
    m_new = jnp.maximum(m, jnp.max(s, axis=-1, keepdims=True))
    alpha = jnp.exp(m - m_new)
    p = jnp.exp(s - m_new)
    l = alpha * l + jnp.sum(p, axis=-1, keepdims=True)
    acc = alpha * acc + _dot(p.astype(BF16), v)
    return m_new, l, acc


def _diff_finish(carry, t, lam, g, lam_init):
    _, l, acc = carry
    o = acc[:t] / l[:t] - lam * (acc[t:] / l[t:])
    return _rms_rows(o, SUBLN_EPS) * g * (1.0 - lam_init)


def _attn_kernel(lam_ref, g_ref, q_ref, k_ref, v_ref, o_ref, *, tq, lam_init):
    qi = pl.program_id(2)
    qs = _split_maps(q_ref[...])

    def block(j):
        off = pl.multiple_of(j * tq, tq)
        return k_ref[pl.ds(off, tq), :], v_ref[pl.ds(off, tq), :]

    def body(j, carry):
        kb, vb = block(j)
        return _online_step(carry, _dot_nt(qs, kb), vb)

    init = (jnp.full((2 * tq, 1), -jnp.inf, F32), jnp.zeros((2 * tq, 1), F32),
            jnp.zeros((2 * tq, HEAD_W), F32))
    carry = lax.fori_loop(0, qi, body, init)
    kb, vb = block(qi)
    s = _dot_nt(qs, kb)
    row = lax.broadcasted_iota(jnp.int32, s.shape, 0) % tq
    col = lax.broadcasted_iota(jnp.int32, s.shape, 1)
    carry = _online_step(carry, jnp.where(col <= row, s, -jnp.inf), vb)
    lam = _diff_lambda(lam_ref, lam_init)
    o_ref[...] = _diff_finish(carry, tq, lam, g_ref[...], lam_init).astype(o_ref.dtype)


def _attn_prompt(p, lamv, g, *, batch, seq, tq, lam_init):
    m = batch * seq
    nq = seq // tq
    return pl.pallas_call(
        functools.partial(_attn_kernel, tq=tq, lam_init=lam_init),
        grid=(batch, N_HEADS, nq),
        in_specs=[
            pl.BlockSpec((4, QK_DIM), lambda b, h, i: (0, 0)),
            pl.BlockSpec((1, HEAD_W), lambda b, h, i: (0, 0)),
            pl.BlockSpec((None, tq, HEAD_W), lambda b, h, i: (SLOT_Q, b * nq + i, h)),
            pl.BlockSpec((None, seq, HEAD_W), lambda b, h, i: (SLOT_K, b, h)),
            pl.BlockSpec((None, seq, HEAD_W), lambda b, h, i: (SLOT_V, b, h)),
        ],
        out_specs=pl.BlockSpec((tq, HEAD_W), lambda b, h, i: (b * nq + i, h)),
        out_shape=jax.ShapeDtypeStruct((m, D_MODEL), BF16),
        compiler_params=_params(("parallel", "parallel", "arbitrary")),
        name="attn_prompt",
    )(lamv, g, p, p, p)


def _paged_kernel(pt_ref, lam_ref, g_ref, q_ref, kn_ref, vn_ref, *refs, pages, t, lam_init):
    k_refs = refs[:pages]
    v_refs = refs[pages:2 * pages]
    o_ref = refs[2 * pages]
    qbd_scr, m_scr, l_scr, acc_scr = refs[2 * pages + 1:]
    step = pl.program_id(1)

    @pl.when(step == 0)
    def _():
        for h in range(N_HEADS):
            qbd_scr[h] = _split_maps(q_ref[:, h * HEAD_W:(h + 1) * HEAD_W]).astype(BF16)
        m_scr[...] = jnp.full(m_scr.shape, -jnp.inf, F32)
        l_scr[...] = jnp.zeros(l_scr.shape, F32)
        acc_scr[...] = jnp.zeros(acc_scr.shape, F32)

    for h in range(N_HEADS):
        qh = qbd_scr[h]
        s = jnp.concatenate([_dot_nt(qh, k_refs[i][:, h, :].astype(BF16)) for i in range(pages)], axis=1)
        m_old = m_scr[h]
        m_new = jnp.maximum(m_old, jnp.max(s, axis=-1, keepdims=True))
        alpha = jnp.exp(m_old - m_new)
        p = jnp.exp(s - m_new[:, :1]).astype(BF16)
        l_scr[h] = alpha * l_scr[h] + jnp.sum(p.astype(F32), axis=-1, keepdims=True)
        pv = _dot(p[:, :PAGE], v_refs[0][:, h, :].astype(BF16))
        for i in range(1, pages):
            pv = pv + _dot(p[:, i * PAGE:(i + 1) * PAGE], v_refs[i][:, h, :].astype(BF16))
        acc_scr[h] = alpha * acc_scr[h] + pv
        m_scr[h] = m_new

    @pl.when(step == pl.num_programs(1) - 1)
    def _():
        lam = _diff_lambda(lam_ref, lam_init)
        for h in range(N_HEADS):
            cols = slice(h * HEAD_W, (h + 1) * HEAD_W)
            qh = qbd_scr[h].astype(F32)
            s = _dot_nt(qh, kn_ref[:, cols])
            row = lax.broadcasted_iota(jnp.int32, s.shape, 0) % t
            col = lax.broadcasted_iota(jnp.int32, s.shape, 1)
            s = jnp.where(col <= row, s, -jnp.inf)
            m_old = m_scr[h][:, :1]
            m_new = jnp.maximum(m_old, jnp.max(s, axis=-1, keepdims=True))
            alpha = jnp.exp(m_old - m_new)
            p = jnp.exp(s - m_new)
            l = alpha * l_scr[h][:, :1] + jnp.sum(p, axis=-1, keepdims=True)
            acc = alpha * acc_scr[h] + _dot(p, vn_ref[:, cols])
            o_ref[:, cols] = _diff_finish((None, l, acc), t, lam, g_ref[...], lam_init)


def _attn_paged(page_table, lamv, g, ps, kvs, cache_k, cache_v, *, layer, pages, lam_init):
    nb, n_pages = page_table.shape
    t = ps.shape[1] // nb
    steps = n_pages // pages

    def page_spec(i):
        return pl.BlockSpec((None, None, PAGE, N_HEADS, HEAD_W),
                            lambda b, s, pt: (layer, pt[b, s * pages + i], 0, 0, 0))

    row_spec = lambda slot: pl.BlockSpec((None, t, D_MODEL), lambda b, s, pt: (slot, b, 0))
    grid_spec = pltpu.PrefetchScalarGridSpec(
        num_scalar_prefetch=1,
        grid=(nb, steps),
        in_specs=[
            pl.BlockSpec((4, QK_DIM), lambda b, s, pt: (0, 0)),
            pl.BlockSpec((1, HEAD_W), lambda b, s, pt: (0, 0)),
            row_spec(SLOT_Q), row_spec(0), row_spec(1),
        ] + [page_spec(i) for i in range(pages)] * 2,
        out_specs=pl.BlockSpec((t, D_MODEL), lambda b, s, pt: (b, 0)),
        scratch_shapes=[
            pltpu.VMEM((N_HEADS, 2 * t, HEAD_W), BF16),
            pltpu.VMEM((N_HEADS, 2 * t, LANES), F32),
            pltpu.VMEM((N_HEADS, 2 * t, LANES), F32),
            pltpu.VMEM((N_HEADS, 2 * t, HEAD_W), F32),
        ],
    )
    return pl.pallas_call(
        functools.partial(_paged_kernel, pages=pages, t=t, lam_init=lam_init),
        grid_spec=grid_spec,
        out_shape=jax.ShapeDtypeStruct((nb * t, D_MODEL), F32),
        compiler_params=_params(("parallel", "arbitrary")),
        name="attn_paged",
    )(page_table, lamv, g, ps, kvs, kvs, *([cache_k] * pages), *([cache_v] * pages))


def _ln_swish_gate(c, ln_g, ln_b, szc):
    mu = jnp.mean(c, axis=-1, keepdims=True)
    xc = c - mu
    var = jnp.mean(xc * xc, axis=-1, keepdims=True)
    y = xc * lax.rsqrt(var + LN_EPS) * ln_g + ln_b
    return _silu(y) * szc


def _conv_prompt_kernel(u_ref, prev_ref, szc_ref, w_ref, b_ref, lg_ref, lb_ref, o_ref, ext_scr, c_scr,
                        *, tc, tiles_per_seq, rb):
    first = (pl.program_id(0) % tiles_per_seq) == 0
    keep = jnp.where(first, 0.0, 1.0)
    ext_scr[0:CONV_HALO, :] = prev_ref[...].astype(F32) * keep
    ext_scr[CONV_HALO:, :] = u_ref[...].astype(F32)
    shift = CONV_HALO - (CONV_K - 1)
    for r in range(tc // rb):
        for c in range(D_MODEL // LANES):
            cols = slice(c * LANES, (c + 1) * LANES)
            acc = jnp.zeros((rb, LANES), F32)
            for j in range(CONV_K):
                lo = r * rb + shift + j
                acc = acc + ext_scr[lo:lo + rb, cols] * w_ref[j:j + 1, cols]
            c_scr[r * rb:(r + 1) * rb, cols] = acc + b_ref[:, cols]
    o_ref[...] = _ln_swish_gate(c_scr[...], lg_ref[...], lb_ref[...], szc_ref[...].astype(F32)).astype(o_ref.dtype)


def _conv_prompt(p, w_dw, b_dw, ln_g, ln_b, *, seq, tc):
    m = p.shape[1]
    halo_blocks = tc // CONV_HALO
    vec = pl.BlockSpec((1, D_MODEL), lambda i: (0, 0))
    return pl.pallas_call(
        functools.partial(_conv_prompt_kernel, tc=tc, tiles_per_seq=seq // tc, rb=32),
        grid=(m // tc,),
        in_specs=[
            pl.BlockSpec((None, tc, D_MODEL), lambda i: (SLOT_U, i, 0)),
            pl.BlockSpec((None, CONV_HALO, D_MODEL), lambda i: (SLOT_U, jnp.maximum(i * halo_blocks - 1, 0), 0)),
            pl.BlockSpec((None, tc, D_MODEL), lambda i: (SLOT_SZC, i, 0)),
            pl.BlockSpec((CONV_K, D_MODEL), lambda i: (0, 0)),
            vec, vec, vec,
        ],
        out_specs=pl.BlockSpec((tc, D_MODEL), lambda i: (i, 0)),
        out_shape=jax.ShapeDtypeStruct((m, D_MODEL), BF16),
        scratch_shapes=[pltpu.VMEM((tc + CONV_HALO, D_MODEL), F32), pltpu.VMEM((tc, D_MODEL), F32)],
        compiler_params=_params(("parallel",)),
        name="conv_prompt",
    )(p, p, p, w_dw, b_dw, ln_g, ln_b)


def _conv_sample_kernel(ext_ref, szc_ref, w_ref, b_ref, lg_ref, lb_ref, o_ref, c_scr, *, nb, t):
    def seq_body(b, _):
        acc = jnp.zeros((t, D_MODEL), F32)
        for j in range(CONV_K):
            acc = acc + ext_ref[b, j:j + t, :] * w_ref[j:j + 1, :]
        c_scr[pl.ds(pl.multiple_of(b * t, t), t), :] = acc + b_ref[...]
        return 0

    lax.fori_loop(0, nb, seq_body, 0)
    o_ref[...] = _ln_swish_gate(c_scr[...], lg_ref[...], lb_ref[...], szc_ref[...]).astype(o_ref.dtype)


def _conv_sample(ext, ps, w_dw, b_dw, ln_g, ln_b):
    nb, rows, _ = ext.shape
    m = ps.shape[1]
    t = m // nb
    vec = pl.BlockSpec((1, D_MODEL), lambda i: (0, 0))
    return pl.pallas_call(
        functools.partial(_conv_sample_kernel, nb=nb, t=t),
        grid=(1,),
        in_specs=[
            pl.BlockSpec((nb, rows, D_MODEL), lambda i: (0, 0, 0)),
            pl.BlockSpec((None, m, D_MODEL), lambda i: (SLOT_SZC, 0, 0)),
            pl.BlockSpec((CONV_K, D_MODEL), lambda i: (0, 0)),
            vec, vec, vec,
        ],
        out_specs=pl.BlockSpec((m, D_MODEL), lambda i: (0, 0)),
        out_shape=jax.ShapeDtypeStruct((m, D_MODEL), F32),
        scratch_shapes=[pltpu.VMEM((m, D_MODEL), F32)],
        compiler_params=_params(("arbitrary",)),
        name="conv_sample",
    )(ext, ps, w_dw, b_dw, ln_g, ln_b)


def _xattn_kernel(q_ref, szx_ref, mk_ref, mv_ref, o_ref):
    outs = []
    for h in range(X_HEADS):
        cols = slice(h * X_DIM, (h + 1) * X_DIM)
        q = q_ref[:, cols].astype(BF16)
        s = _dot_nt(q, mk_ref[:, cols].astype(BF16))
        p = jnp.exp(s - jnp.max(s, axis=-1, keepdims=True))
        o = _dot(p.astype(BF16), mv_ref[:, cols].astype(BF16))
        outs.append(o / jnp.sum(p, axis=-1, keepdims=True))
    o_ref[...] = (jnp.concatenate(outs, axis=1) * szx_ref[...].astype(F32)).astype(o_ref.dtype)


def _xattn(p, mem_k, mem_v, *, tx, out_dtype):
    m = p.shape[1]
    groups = mem_k.shape[0]
    tiles = m // groups // tx
    mem_spec = pl.BlockSpec((None, N_MEM, D_MODEL), lambda g, i: (g, 0, 0))
    return pl.pallas_call(
        _xattn_kernel,
        grid=(groups, tiles),
        in_specs=[
            pl.BlockSpec((None, tx, D_MODEL), lambda g, i: (SLOT_QX, g * tiles + i, 0)),
            pl.BlockSpec((None, tx, D_MODEL), lambda g, i: (SLOT_SZX, g * tiles + i, 0)),
            mem_spec, mem_spec,
        ],
        out_specs=pl.BlockSpec((tx, D_MODEL), lambda g, i: (g * tiles + i, 0)),
        out_shape=jax.ShapeDtypeStruct((m, D_MODEL), out_dtype),
        compiler_params=_params(("parallel", "arbitrary")),
        name="xattn",
    )(p, p, mem_k, mem_v)


def _merge_kernel(cg_ref, on_ref, sza_ref, xg_ref, sgc_ref, sga_ref, sgx_ref, x_ref,
                  wpc_ref, wpa_ref, wpx_ref, wo_ref, fg_ref, o_ref, *, final):
    f = lambda r: r[...].astype(F32)
    yc = _dot(cg_ref[...].astype(BF16), wpc_ref[...])
    ya = _dot((f(on_ref) * f(sza_ref)).astype(BF16), wpa_ref[...])
    yx = _dot(xg_ref[...].astype(BF16), wpx_ref[...])
    merged = f(sgc_ref) * yc + f(sga_ref) * ya + f(sgx_ref) * yx
    x_new = x_ref[...] + _dot(merged.astype(BF16), wo_ref[...])
    if final:
        x_new = _rms_rows(x_new, RMS_EPS) * fg_ref[...]
    o_ref[...] = x_new


def _merge(cg, on, xg, p, x, w_pc, w_pa, w_px, w_o, final_g, *, tm, final):
    m = x.shape[0]
    row = pl.BlockSpec((tm, D_MODEL), lambda i: (i, 0))
    slot = lambda s: pl.BlockSpec((None, tm, D_MODEL), lambda i: (s, i, 0))
    wspec = pl.BlockSpec((D_MODEL, D_MODEL), lambda i: (0, 0))
    return pl.pallas_call(
        functools.partial(_merge_kernel, final=final),
        grid=(m // tm,),
        in_specs=[row, row, slot(SLOT_SZA), row, slot(SLOT_SGC), slot(SLOT_SGA), slot(SLOT_SGX), row,
                  wspec, wspec, wspec, wspec, pl.BlockSpec((1, D_MODEL), lambda i: (0, 0))],
        out_specs=row,
        out_shape=jax.ShapeDtypeStruct((m, D_MODEL), F32),
        compiler_params=_params(("parallel",)),
        name="merge",
    )(cg, on, p, xg, p, p, p, x, w_pc, w_pa, w_px, w_o, final_g)


def _rope_tables(pos):
    half = ROPE_DIM // 2
    inv = ROPE_THETA ** (-jnp.arange(half, dtype=F32) * 2.0 / ROPE_DIM)
    ang = pos.astype(F32)[:, None] * inv[None, :]
    cos, sin = jnp.cos(ang), jnp.sin(ang)
    n = pos.shape[0]
    pad = jnp.zeros((n, QK_DIM - ROPE_DIM), F32)
    zero = jnp.zeros((n, half), F32)
    cos64 = jnp.concatenate([cos, cos, pad + 1.0], axis=1)
    lo64 = jnp.concatenate([-sin, zero, pad], axis=1)
    hi64 = jnp.concatenate([zero, sin, pad], axis=1)
    rep = lambda a: jnp.tile(a, (1, LANES // QK_DIM))
    return rep(cos64), rep(lo64), rep(hi64)


def kernel(x_prompt, x_sample, cache_k, cache_v, cache_conv, cache_mem_k, cache_mem_v, page_table, mem_prompt,
           norm_g, w_in, w_dw, b_dw, conv_ln_g, conv_ln_b, w_pc, lam_q1, lam_k1, lam_q2, lam_k2, subln_g,
           w_pa, mem_norm_g, w_mk, w_mv, w_px, w_o, final_g):
    batch, seq, _ = x_prompt.shape
    nb, t, _ = x_sample.shape
    depth = w_in.shape[0]
    past_len = page_table.shape[1] * PAGE
    mp, ms = batch * seq, nb * t

    tabs_p = _rope_tables(jnp.arange(seq))
    tabs_s = tuple(jnp.tile(a, (nb, 1)) for a in _rope_tables(past_len + jnp.arange(t)))

    xp = x_prompt.reshape(mp, D_MODEL)
    xs = x_sample.reshape(ms, D_MODEL)
    mem = mem_prompt.reshape(batch * N_MEM, D_MODEL)
    fg = final_g.reshape(1, D_MODEL)
    vec = lambda a: a.reshape(1, -1)

    tm_p = 512
    outs = {k: [] for k in ("kp", "vp", "cp", "mkp", "mvp", "ks", "vs", "cs")}
    for l in range(depth):
        lam_init = 0.8 - 0.6 * math.exp(-0.3 * l)
        final = l == depth - 1
        lamv = jnp.stack([lam_q1[l], lam_k1[l], lam_q2[l], lam_k2[l]])
        w_in_l = w_in[l].astype(BF16)
        wpc, wpa, wpx, wo = (w[l].astype(BF16) for w in (w_pc, w_pa, w_px, w_o))
        g_sub = vec(subln_g[l])
        conv_args = (w_dw[l], vec(b_dw[l]), vec(conv_ln_g[l]), vec(conv_ln_b[l]))

        mkv = _memkv(mem, vec(mem_norm_g[l]), jnp.concatenate([w_mk[l], w_mv[l]], axis=1).astype(BF16), tm=512)
        mk_p = mkv[0].reshape(batch, N_MEM, D_MODEL)
        mv_p = mkv[1].reshape(batch, N_MEM, D_MODEL)
        pp, kvp = _proj(xp, vec(norm_g[l]), w_in_l, *tabs_p, tm=tm_p, table_blocks=seq // tm_p, out_dtype=BF16)
        on_p = _attn_prompt(pp, lamv, g_sub, batch=batch, seq=seq, tq=256, lam_init=lam_init)
        cg_p = _conv_prompt(pp, *conv_args, seq=seq, tc=128)
        xg_p = _xattn(pp, mk_p, mv_p, tx=512, out_dtype=BF16)
        xp = _merge(cg_p, on_p, xg_p, pp, xp, wpc, wpa, wpx, wo, fg, tm=256, final=final)

        ps, kvs = _proj(xs, vec(norm_g[l]), w_in_l, *tabs_s, tm=ms, table_blocks=1, out_dtype=F32)
        on_s = _attn_paged(page_table, lamv, g_sub, ps, kvs, cache_k, cache_v, layer=l, pages=8, lam_init=lam_init)
        u_s = ps[SLOT_U].reshape(nb, t, D_MODEL)
        ext_s = jnp.concatenate([cache_conv[l], u_s, jnp.zeros((nb, 2, D_MODEL), F32)], axis=1)
        cg_s = _conv_sample(ext_s, ps, *conv_args)
        xg_s = _xattn(ps, cache_mem_k[l].reshape(nb, N_MEM, D_MODEL), cache_mem_v[l].reshape(nb, N_MEM, D_MODEL),
                      tx=t, out_dtype=F32)
        xs = _merge(cg_s, on_s, xg_s, ps, xs, wpc, wpa, wpx, wo, fg, tm=ms, final=final)

        outs["kp"].append(kvp[0].reshape(batch, seq, N_HEADS, HEAD_W))
        outs["vp"].append(kvp[1].reshape(batch, seq, N_HEADS, HEAD_W))
        outs["cp"].append(pp[SLOT_U].reshape(batch, seq, D_MODEL)[:, seq - (CONV_K - 1):].astype(F32))
        outs["mkp"].append(mk_p.reshape(batch, N_MEM, X_HEADS, X_DIM))
        outs["mvp"].append(mv_p.reshape(batch, N_MEM, X_HEADS, X_DIM))
        outs["ks"].append(kvs[0].reshape(nb, t, N_HEADS, HEAD_W))
        outs["vs"].append(kvs[1].reshape(nb, t, N_HEADS, HEAD_W))
        outs["cs"].append(ext_s[:, t:t + CONV_K - 1])

    st = lambda k: jnp.stack(outs[k])
    return (xp.reshape(batch, seq, D_MODEL), xs.reshape(nb, t, D_MODEL), st("kp"), st("vp"), st("cp"),
            st("mkp"), st("mvp"), st("ks"), st("vs"), st("cs"))
```

```python
import functools
import math

import jax
import jax.numpy as jnp
from jax import lax
from jax.experimental import pallas as pl
from jax.experimental.pallas import tpu as pltpu

D_MODEL = 1024
N_HEADS = 8
QK_DIM = 64
HEAD_W = 128
ROPE_DIM = 16
ROPE_THETA = 500000.0
CONV_K = 31
X_HEADS = 4
X_DIM = 256
N_MEM = 256
N_SPLIT = 12
PAGE = 128
RMS_EPS = 1e-6
LN_EPS = 1e-5
SUBLN_EPS = 1e-5
LANES = 128
SUBLANES = 8
CONV_HALO = 32
LOG2E = 1.4426950408889634

VMEM_LIMIT = 52 * 1024 * 1024

TM_PROJ = 512
T_ATTN = 256
HEADS_PER_STEP = 4
TC_CONV = 256
TX_XATTN = 512
TM_MERGE = 256
PAGES_PER_STEP = 8

SLOT_U, SLOT_SZC, SLOT_Q, SLOT_K, SLOT_V, SLOT_SZA, SLOT_QX, SLOT_SZX, SLOT_SGC, SLOT_SGA, SLOT_SGX = range(11)
N_SLOTS = 11

BF16 = jnp.bfloat16
F32 = jnp.float32


def _params(sem):
    return pltpu.CompilerParams(dimension_semantics=sem, vmem_limit_bytes=VMEM_LIMIT)


def _sigmoid(x):
    return 0.5 * jnp.tanh(0.5 * x) + 0.5


def _silu(x):
    return x * _sigmoid(x)


def _rms_rows(x, eps):
    return x * lax.rsqrt(jnp.mean(x * x, axis=-1, keepdims=True) + eps)


def _dot(a, b):
    return jnp.dot(a, b, preferred_element_type=F32)


def _dot_nt(a, b):
    return lax.dot_general(a, b, (((1,), (1,)), ((), ())), preferred_element_type=F32)


def _diff_lambda(lam_ref, lam_init):
    lv = lam_ref[...]
    e1 = jnp.exp(jnp.sum(lv[0:1] * lv[1:2], axis=-1, keepdims=True))
    e2 = jnp.exp(jnp.sum(lv[2:3] * lv[3:4], axis=-1, keepdims=True))
    return e1 - e2 + lam_init


def _head_cols(h):
    return slice(h * HEAD_W, (h + 1) * HEAD_W)


def _head_rows(h, n):
    return pl.ds(h, n, stride=N_HEADS)


def _rope_cols(acc, cos, sin_lo, sin_hi):
    outs = []
    for c in range(D_MODEL // LANES):
        xc = acc[:, c * LANES:(c + 1) * LANES]
        outs.append(xc * cos + pltpu.roll(xc, LANES - 8, 1) * sin_lo + pltpu.roll(xc, 8, 1) * sin_hi)
    return jnp.concatenate(outs, axis=1)


def _proj_kernel(*refs, tm, tk, n_prev):
    x_ref, g_ref, w_ref, cos_ref, slo_ref, shi_ref = refs[:6]
    refs = refs[6 + n_prev:]
    p_ref, k_ref, v_ref = refs[:3]
    vt_ref = refs[3] if tk else None
    h_scr, a_scr = refs[-2:]
    j = pl.program_id(1)
    dt = p_ref.dtype

    @pl.when(j == 0)
    def _():
        h_scr[...] = (_rms_rows(x_ref[...], RMS_EPS) * g_ref[...]).astype(BF16)

    def acc():
        return _dot(h_scr[...], w_ref[...])

    def rope(x):
        return _rope_cols(x, cos_ref[...], slo_ref[...], shi_ref[...])

    @pl.when(j == 0)
    def _():
        a_scr[...] = acc()

    @pl.when(j == 1)
    def _():
        p_ref[...] = (a_scr[...] * _sigmoid(acc())).astype(dt)

    @pl.when((j == 2) | (j == 6) | (j == 8))
    def _():
        p_ref[...] = _silu(acc()).astype(dt)

    @pl.when(j == 3)
    def _():
        p_ref[...] = (rope(acc()) * (QK_DIM ** -0.5 * LOG2E)).astype(dt)

    @pl.when(j == 4)
    def _():
        k = rope(acc())
        p_ref[...] = k.astype(dt)
        for h in range(N_HEADS):
            k_ref[_head_rows(h, tm), :] = k[:, _head_cols(h)]

    @pl.when(j == 5)
    def _():
        v = acc()
        p_ref[...] = v.astype(dt)
        for h in range(N_HEADS):
            vh = v[:, _head_cols(h)]
            v_ref[_head_rows(h, tm), :] = vh
            if tk:
                for c in range(tm // tk):
                    vt_ref[h, c] = vh[c * tk:(c + 1) * tk, :].T.astype(BF16)

    @pl.when(j == 7)
    def _():
        p_ref[...] = (acc() * (X_DIM ** -0.5)).astype(dt)

    @pl.when(j >= 9)
    def _():
        p_ref[...] = _sigmoid(acc()).astype(dt)


def _proj(x, g, w, cos, slo, shi, kv_prev, *, layer, depth, tm, table_blocks, out_dtype, tk):
    m = x.shape[0]
    tab_spec = pl.BlockSpec((tm, LANES), lambda i, j: (i % table_blocks, 0))
    kv_spec = pl.BlockSpec((None, tm * N_HEADS, HEAD_W), lambda i, j: (layer, i, 0))
    kv_shape = jax.ShapeDtypeStruct((depth, m * N_HEADS, HEAD_W), F32)
    in_specs = [
        pl.BlockSpec((tm, D_MODEL), lambda i, j: (i, 0)),
        pl.BlockSpec((1, D_MODEL), lambda i, j: (0, 0)),
        pl.BlockSpec((D_MODEL, D_MODEL), lambda i, j: (0, j)),
        tab_spec, tab_spec, tab_spec,
    ] + [pl.BlockSpec(memory_space=pl.ANY)] * len(kv_prev)
    out_specs = [pl.BlockSpec((None, tm, D_MODEL), lambda i, j: (jnp.maximum(j - 1, 0), i, 0)), kv_spec, kv_spec]
    out_shape = [jax.ShapeDtypeStruct((N_SLOTS, m, D_MODEL), out_dtype), kv_shape, kv_shape]
    if tk:
        out_specs.append(pl.BlockSpec((N_HEADS, tm // tk, HEAD_W, tk), lambda i, j: (0, i, 0, 0)))
        out_shape.append(jax.ShapeDtypeStruct((N_HEADS, m // tk, HEAD_W, tk), BF16))
    return pl.pallas_call(
        functools.partial(_proj_kernel, tm=tm, tk=tk, n_prev=len(kv_prev)),
        grid=(m // tm, N_SPLIT),
        in_specs=in_specs,
        out_specs=out_specs,
        out_shape=out_shape,
        scratch_shapes=[pltpu.VMEM((tm, D_MODEL), BF16), pltpu.VMEM((tm, D_MODEL), F32)],
        input_output_aliases={6 + n: 1 + n for n in range(len(kv_prev))},
        compiler_params=_params(("parallel", "arbitrary")),
        name="proj",
    )(x, g, w, cos, slo, shi, *kv_prev)


def _memkv_kernel(x_ref, g_ref, w_ref, o_ref, h_scr):
    @pl.when(pl.program_id(1) == 0)
    def _():
        h_scr[...] = (_rms_rows(x_ref[...], RMS_EPS) * g_ref[...]).astype(BF16)

    o_ref[...] = _dot(h_scr[...], w_ref[...])


def _memkv(mem, g, w, *, tm):
    m = mem.shape[0]
    n_groups = w.shape[1] // D_MODEL
    return pl.pallas_call(
        _memkv_kernel,
        grid=(m // tm, n_groups),
        in_specs=[
            pl.BlockSpec((tm, D_MODEL), lambda i, j: (i, 0)),
            pl.BlockSpec((1, D_MODEL), lambda i, j: (0, 0)),
            pl.BlockSpec((D_MODEL, D_MODEL), lambda i, j: (0, j)),
        ],
        out_specs=pl.BlockSpec((None, tm, D_MODEL), lambda i, j: (j, i, 0)),
        out_shape=jax.ShapeDtypeStruct((n_groups, m, D_MODEL), F32),
        scratch_shapes=[pltpu.VMEM((tm, D_MODEL), BF16)],
        compiler_params=_params(("parallel", "arbitrary")),
        name="memkv",
    )(mem, g, w)


def _split_maps(q):
    lane = lax.broadcasted_iota(jnp.int32, q.shape, 1)
    zero = jnp.zeros_like(q)
    return jnp.concatenate([jnp.where(lane < QK_DIM, q, zero), jnp.where(lane >= QK_DIM, q, zero)], axis=0)


def _attn_kernel(lam_ref, g_ref, q_ref, k_ref, vt_ref, o_ref, sa_scr, sb_scr, acc_scr, *, t, hp, lam_init):
    qi = pl.program_id(2)
    qs = [_split_maps(q_ref[:, _head_cols(h)]) for h in range(hp)]

    def scores(h, j):
        off = pl.multiple_of(j * t, t)
        return _dot_nt(k_ref[pl.ds(off, t), _head_cols(h)], qs[h])

    def update(h, j, st, stats):
        m, l = stats
        m_new = jnp.maximum(m, jnp.max(st, axis=0, keepdims=True))
        alpha = jnp.exp2(m - m_new)
        p = jnp.exp2(st - m_new)
        l_new = alpha * l + jnp.sum(p, axis=0, keepdims=True)
        acc_scr[h] = alpha * acc_scr[h] + _dot(vt_ref[h, j], p.astype(BF16))
        return m_new, l_new

    def full_step(j, src, dst, carry):
        out = []
        for h in range(hp):
            st = src[h]
            dst[h] = scores(h, j + 1)
            out.append(update(h, j, st, carry[h]))
        return tuple(out)

    acc_scr[...] = jnp.zeros(acc_scr.shape, F32)
    for h in range(hp):
        sa_scr[h] = scores(h, 0)
    init = tuple((jnp.full((1, 2 * t), -jnp.inf, F32), jnp.zeros((1, 2 * t), F32)) for _ in range(hp))

    def pair(jj, carry):
        carry = full_step(2 * jj, sa_scr, sb_scr, carry)
        return full_step(2 * jj + 1, sb_scr, sa_scr, carry)

    carry = lax.fori_loop(0, qi // 2, pair, init)
    odd = (qi % 2) == 1
    carry = lax.cond(odd, lambda c: full_step(qi - 1, sa_scr, sb_scr, c), lambda c: c, carry)

    lam = _diff_lambda(lam_ref, lam_init)
    key = lax.broadcasted_iota(jnp.int32, (t, 2 * t), 0)
    qry = lax.broadcasted_iota(jnp.int32, (t, 2 * t), 1) % t
    for h in range(hp):
        st = jnp.where(odd, sb_scr[h], sa_scr[h])
        _, l = update(h, qi, jnp.where(key <= qry, st, -jnp.inf), carry[h])
        acc = acc_scr[h]
        o = acc[:, :t] / l[:, :t] - lam * (acc[:, t:] / l[:, t:])
        o = o * lax.rsqrt(jnp.mean(o * o, axis=0, keepdims=True) + SUBLN_EPS)
        o_ref[:, _head_cols(h)] = (o.T * g_ref[...] * (1.0 - lam_init)).astype(o_ref.dtype)


def _attn_prompt(p, vt, lamv, g, *, batch, seq, t, hp, lam_init):
    m = batch * seq
    nq = seq // t
    w = hp * HEAD_W
    return pl.pallas_call(
        functools.partial(_attn_kernel, t=t, hp=hp, lam_init=lam_init),
        grid=(batch, N_HEADS // hp, nq),
        in_specs=[
            pl.BlockSpec((4, QK_DIM), lambda b, h, i: (0, 0)),
            pl.BlockSpec((1, HEAD_W), lambda b, h, i: (0, 0)),
            pl.BlockSpec((None, t, w), lambda b, h, i: (SLOT_Q, b * nq + i, h)),
            pl.BlockSpec((None, seq, w), lambda b, h, i: (SLOT_K, b, h)),
            pl.BlockSpec((hp, nq, HEAD_W, t), lambda b, h, i: (h, b, 0, 0)),
        ],
        out_specs=pl.BlockSpec((t, w), lambda b, h, i: (b * nq + i, h)),
        out_shape=jax.ShapeDtypeStruct((m, D_MODEL), BF16),
        scratch_shapes=[
            pltpu.VMEM((hp, t, 2 * t), F32),
            pltpu.VMEM((hp, t, 2 * t), F32),
            pltpu.VMEM((hp, HEAD_W, 2 * t), F32),
        ],
        compiler_params=_params(("parallel", "parallel", "arbitrary")),
        name="attn_prompt",
    )(lamv, g, p, p, vt)


def _paged_kernel(pt_ref, lam_ref, g_ref, q_ref, kn_ref, vn_ref, *refs, pages, t, lam_init):
    k_refs = refs[:pages]
    v_refs = refs[pages:2 * pages]
    o_ref = refs[2 * pages]
    qbd_scr, m_scr, l_scr, acc_scr = refs[2 * pages + 1:]
    step = pl.program_id(1)

    @pl.when(step == 0)
    def _():
        for h in range(N_HEADS):
            qbd_scr[h] = _split_maps(q_ref[:, _head_cols(h)]).astype(BF16)
        m_scr[...] = jnp.full(m_scr.shape, -jnp.inf, F32)
        l_scr[...] = jnp.zeros(l_scr.shape, F32)
        acc_scr[...] = jnp.zeros(acc_scr.shape, F32)

    s = jnp.concatenate(
        [jnp.concatenate([_dot_nt(qbd_scr[h], k_refs[i][_head_rows(h, PAGE), :].astype(BF16))
                          for i in range(pages)], axis=1) for h in range(N_HEADS)], axis=0)
    m_old = m_scr[...]
    m_new = jnp.maximum(m_old, jnp.max(s, axis=-1, keepdims=True))
    alpha = jnp.exp2(m_old - m_new)
    p = jnp.exp2(s - m_new[:, :1])
    l_scr[...] = alpha * l_scr[...] + jnp.sum(p, axis=-1, keepdims=True)
    m_scr[...] = m_new
    p = p.astype(BF16)
    pvs = []
    for h in range(N_HEADS):
        ph = p[h * 2 * t:(h + 1) * 2 * t]
        pv = _dot(ph[:, :PAGE], v_refs[0][_head_rows(h, PAGE), :].astype(BF16))
        for i in range(1, pages):
            pv = pv + _dot(ph[:, i * PAGE:(i + 1) * PAGE], v_refs[i][_head_rows(h, PAGE), :].astype(BF16))
        pvs.append(pv)
    acc_scr[...] = alpha * acc_scr[...] + jnp.concatenate(pvs, axis=0)

    @pl.when(step == pl.num_programs(1) - 1)
    def _():
        lam = _diff_lambda(lam_ref, lam_init)
        for h in range(N_HEADS):
            rows = _head_rows(h, t)
            qh = qbd_scr[h].astype(F32)
            s = _dot_nt(qh, kn_ref[rows, :])
            row = lax.broadcasted_iota(jnp.int32, s.shape, 0) % t
            col = lax.broadcasted_iota(jnp.int32, s.shape, 1)
            s = jnp.where(col <= row, s, -jnp.inf)
            hr = slice(h * 2 * t, (h + 1) * 2 * t)
            m_old = m_scr[hr, :1]
            m_new = jnp.maximum(m_old, jnp.max(s, axis=-1, keepdims=True))
            alpha = jnp.exp2(m_old - m_new)
            p = jnp.exp2(s - m_new)
            l = alpha * l_scr[hr, :1] + jnp.sum(p, axis=-1, keepdims=True)
            acc = alpha * acc_scr[hr, :] + _dot(p, vn_ref[rows, :])
            o = acc[:t] / l[:t] - lam * (acc[t:] / l[t:])
            o_ref[:, _head_cols(h)] = _rms_rows(o, SUBLN_EPS) * g_ref[...] * (1.0 - lam_init)


def _attn_paged(page_table, lamv, g, ps, k_all, v_all, cache_k, cache_v, *, layer, pages, lam_init):
    nb, n_pages = page_table.shape
    t = ps.shape[1] // nb
    steps = n_pages // pages

    def page_spec(i):
        return pl.BlockSpec((None, None, PAGE * N_HEADS, HEAD_W),
                            lambda b, s, pt: (layer, pt[b, s * pages + i], 0, 0))

    new_spec = pl.BlockSpec((None, t * N_HEADS, HEAD_W), lambda b, s, pt: (layer, b, 0))
    grid_spec = pltpu.PrefetchScalarGridSpec(
        num_scalar_prefetch=1,
        grid=(nb, steps),
        in_specs=[
            pl.BlockSpec((4, QK_DIM), lambda b, s, pt: (0, 0)),
            pl.BlockSpec((1, HEAD_W), lambda b, s, pt: (0, 0)),
            pl.BlockSpec((None, t, D_MODEL), lambda b, s, pt: (SLOT_Q, b, 0)),
            new_spec, new_spec,
        ] + [page_spec(i) for i in range(pages)] * 2,
        out_specs=pl.BlockSpec((t, D_MODEL), lambda b, s, pt: (b, 0)),
        scratch_shapes=[
            pltpu.VMEM((N_HEADS, 2 * t, HEAD_W), BF16),
            pltpu.VMEM((N_HEADS * 2 * t, LANES), F32),
            pltpu.VMEM((N_HEADS * 2 * t, LANES), F32),
            pltpu.VMEM((N_HEADS * 2 * t, HEAD_W), F32),
        ],
    )
    return pl.pallas_call(
        functools.partial(_paged_kernel, pages=pages, t=t, lam_init=lam_init),
        grid_spec=grid_spec,
        out_shape=jax.ShapeDtypeStruct((nb * t, D_MODEL), F32),
        compiler_params=_params(("parallel", "arbitrary")),
        name="attn_paged",
    )(page_table, lamv, g, ps, k_all, v_all, *([cache_k] * pages), *([cache_v] * pages))


def _ln_swish_gate(c, ln_g, ln_b, szc):
    mu = jnp.mean(c, axis=-1, keepdims=True)
    xc = c - mu
    var = jnp.mean(xc * xc, axis=-1, keepdims=True)
    y = xc * lax.rsqrt(var + LN_EPS) * ln_g + ln_b
    return _silu(y) * szc


def _conv_prompt_kernel(u_ref, prev_ref, szc_ref, w_ref, b_ref, lg_ref, lb_ref, o_ref, ext_scr, c_scr,
                        *, tc, tiles_per_seq, rb):
    first = (pl.program_id(0) % tiles_per_seq) == 0
    keep = jnp.where(first, 0.0, 1.0)
    ext_scr[0:CONV_HALO, :] = prev_ref[...].astype(F32) * keep
    ext_scr[CONV_HALO:CONV_HALO + tc, :] = u_ref[...].astype(F32)
    ext_scr[CONV_HALO + tc:, :] = jnp.zeros((SUBLANES, D_MODEL), F32)
    shift = CONV_HALO - (CONV_K - 1)
    for r in range(tc // rb):
        for c in range(D_MODEL // LANES):
            cols = slice(c * LANES, (c + 1) * LANES)
            acc = jnp.zeros((rb, LANES), F32)
            for b in range(SUBLANES):
                z = jnp.zeros((rb + SUBLANES, LANES), F32)
                for a in range((shift + CONV_K - 1) // SUBLANES + 1):
                    j = a * SUBLANES + b - shift
                    if 0 <= j < CONV_K:
                        lo = r * rb + a * SUBLANES
                        z = z + ext_scr[lo:lo + rb + SUBLANES, cols] * w_ref[j:j + 1, cols]
                acc = acc + z[b:b + rb]
            c_scr[r * rb:(r + 1) * rb, cols] = acc + b_ref[:, cols]
    o_ref[...] = _ln_swish_gate(c_scr[...], lg_ref[...], lb_ref[...], szc_ref[...].astype(F32)).astype(o_ref.dtype)


def _conv_prompt(p, w_dw, b_dw, ln_g, ln_b, *, seq, tc):
    m = p.shape[1]
    halo_blocks = tc // CONV_HALO
    vec = pl.BlockSpec((1, D_MODEL), lambda i: (0, 0))
    return pl.pallas_call(
        functools.partial(_conv_prompt_kernel, tc=tc, tiles_per_seq=seq // tc, rb=64),
        grid=(m // tc,),
        in_specs=[
            pl.BlockSpec((None, tc, D_MODEL), lambda i: (SLOT_U, i, 0)),
            pl.BlockSpec((None, CONV_HALO, D_MODEL), lambda i: (SLOT_U, jnp.maximum(i * halo_blocks - 1, 0), 0)),
            pl.BlockSpec((None, tc, D_MODEL), lambda i: (SLOT_SZC, i, 0)),
            pl.BlockSpec((CONV_K, D_MODEL), lambda i: (0, 0)),
            vec, vec, vec,
        ],
        out_specs=pl.BlockSpec((tc, D_MODEL), lambda i: (i, 0)),
        out_shape=jax.ShapeDtypeStruct((m, D_MODEL), BF16),
        scratch_shapes=[pltpu.VMEM((tc + CONV_HALO + SUBLANES, D_MODEL), F32), pltpu.VMEM((tc, D_MODEL), F32)],
        compiler_params=_params(("parallel",)),
        name="conv_prompt",
    )(p, p, p, w_dw, b_dw, ln_g, ln_b)


def _conv_sample_kernel(ext_ref, szc_ref, w_ref, b_ref, lg_ref, lb_ref, o_ref, c_scr, *, nb, t):
    def seq_body(b, _):
        acc = jnp.zeros((t, D_MODEL), F32)
        for j in range(CONV_K):
            acc = acc + ext_ref[b, j:j + t, :] * w_ref[j:j + 1, :]
        c_scr[pl.ds(pl.multiple_of(b * t, t), t), :] = acc + b_ref[...]
        return 0

    lax.fori_loop(0, nb, seq_body, 0)
    o_ref[...] = _ln_swish_gate(c_scr[...], lg_ref[...], lb_ref[...], szc_ref[...]).astype(o_ref.dtype)


def _conv_sample(ext, ps, w_dw, b_dw, ln_g, ln_b):
    nb, rows, _ = ext.shape
    m = ps.shape[1]
    t = m // nb
    vec = pl.BlockSpec((1, D_MODEL), lambda i: (0, 0))
    return pl.pallas_call(
        functools.partial(_conv_sample_kernel, nb=nb, t=t),
        grid=(1,),
        in_specs=[
            pl.BlockSpec((nb, rows, D_MODEL), lambda i: (0, 0, 0)),
            pl.BlockSpec((None, m, D_MODEL), lambda i: (SLOT_SZC, 0, 0)),
            pl.BlockSpec((CONV_K, D_MODEL), lambda i: (0, 0)),
            vec, vec, vec,
        ],
        out_specs=pl.BlockSpec((m, D_MODEL), lambda i: (0, 0)),
        out_shape=jax.ShapeDtypeStruct((m, D_MODEL), F32),
        scratch_shapes=[pltpu.VMEM((m, D_MODEL), F32)],
        compiler_params=_params(("arbitrary",)),
        name="conv_sample",
    )(ext, ps, w_dw, b_dw, ln_g, ln_b)


def _xattn_kernel(q_ref, szx_ref, mk_ref, mv_ref, o_ref):
    outs = []
    for h in range(X_HEADS):
        cols = slice(h * X_DIM, (h + 1) * X_DIM)
        q = q_ref[:, cols].astype(BF16)
        s = _dot_nt(q, mk_ref[:, cols].astype(BF16))
        p = jnp.exp(s - jnp.max(s, axis=-1, keepdims=True))
        o = _dot(p.astype(BF16), mv_ref[:, cols].astype(BF16))
        outs.append(o / jnp.sum(p, axis=-1, keepdims=True))
    o_ref[...] = (jnp.concatenate(outs, axis=1) * szx_ref[...].astype(F32)).astype(o_ref.dtype)


def _xattn(p, mem_k, mem_v, *, tx, out_dtype):
    m = p.shape[1]
    groups = mem_k.shape[0]
    tiles = m // groups // tx
    mem_spec = pl.BlockSpec((None, N_MEM, D_MODEL), lambda g, i: (g, 0, 0))
    return pl.pallas_call(
        _xattn_kernel,
        grid=(groups, tiles),
        in_specs=[
            pl.BlockSpec((None, tx, D_MODEL), lambda g, i: (SLOT_QX, g * tiles + i, 0)),
            pl.BlockSpec((None, tx, D_MODEL), lambda g, i: (SLOT_SZX, g * tiles + i, 0)),
            mem_spec, mem_spec,
        ],
        out_specs=pl.BlockSpec((tx, D_MODEL), lambda g, i: (g * tiles + i, 0)),
        out_shape=jax.ShapeDtypeStruct((m, D_MODEL), out_dtype),
        compiler_params=_params(("parallel", "arbitrary")),
        name="xattn",
    )(p, p, mem_k, mem_v)


def _merge_kernel(cg_ref, on_ref, sza_ref, xg_ref, sgc_ref, sga_ref, sgx_ref, x_ref,
                  wpc_ref, wpa_ref, wpx_ref, wo_ref, fg_ref, o_ref, *, final):
    f = lambda r: r[...].astype(F32)
    yc = _dot(cg_ref[...].astype(BF16), wpc_ref[...])
    ya = _dot((f(on_ref) * f(sza_ref)).astype(BF16), wpa_ref[...])
    yx = _dot(xg_ref[...].astype(BF16), wpx_ref[...])
    merged = f(sgc_ref) * yc + f(sga_ref) * ya + f(sgx_ref) * yx
    x_new = x_ref[...] + _dot(merged.astype(BF16), wo_ref[...])
    if final:
        x_new = _rms_rows(x_new, RMS_EPS) * fg_ref[...]
    o_ref[...] = x_new


def _merge(cg, on, xg, p, x, w_pc, w_pa, w_px, w_o, final_g, *, tm, final):
    m = x.shape[0]
    row = pl.BlockSpec((tm, D_MODEL), lambda i: (i, 0))
    slot = lambda s: pl.BlockSpec((None, tm, D_MODEL), lambda i: (s, i, 0))
    wspec = pl.BlockSpec((D_MODEL, D_MODEL), lambda i: (0, 0))
    return pl.pallas_call(
        functools.partial(_merge_kernel, final=final),
        grid=(m // tm,),
        in_specs=[row, row, slot(SLOT_SZA), row, slot(SLOT_SGC), slot(SLOT_SGA), slot(SLOT_SGX), row,
                  wspec, wspec, wspec, wspec, pl.BlockSpec((1, D_MODEL), lambda i: (0, 0))],
        out_specs=row,
        out_shape=jax.ShapeDtypeStruct((m, D_MODEL), F32),
        compiler_params=_params(("parallel",)),
        name="merge",
    )(cg, on, p, xg, p, p, p, x, w_pc, w_pa, w_px, w_o, final_g)


def _rope_tables(pos):
    half = ROPE_DIM // 2
    inv = ROPE_THETA ** (-jnp.arange(half, dtype=F32) * 2.0 / ROPE_DIM)
    ang = pos.astype(F32)[:, None] * inv[None, :]
    cos, sin = jnp.cos(ang), jnp.sin(ang)
    n = pos.shape[0]
    pad = jnp.zeros((n, QK_DIM - ROPE_DIM), F32)
    zero = jnp.zeros((n, half), F32)
    cos64 = jnp.concatenate([cos, cos, pad + 1.0], axis=1)
    lo64 = jnp.concatenate([-sin, zero, pad], axis=1)
    hi64 = jnp.concatenate([zero, sin, pad], axis=1)
    rep = lambda a: jnp.tile(a, (1, LANES // QK_DIM))
    return rep(cos64), rep(lo64), rep(hi64)


def kernel(x_prompt, x_sample, cache_k, cache_v, cache_conv, cache_mem_k, cache_mem_v, page_table, mem_prompt,
           norm_g, w_in, w_dw, b_dw, conv_ln_g, conv_ln_b, w_pc, lam_q1, lam_k1, lam_q2, lam_k2, subln_g,
           w_pa, mem_norm_g, w_mk, w_mv, w_px, w_o, final_g):
    batch, seq, _ = x_prompt.shape
    nb, t, _ = x_sample.shape
    depth = w_in.shape[0]
    n_phys = cache_k.shape[1]
    past_len = page_table.shape[1] * PAGE
    mp, ms = batch * seq, nb * t

    tabs_p = _rope_tables(jnp.arange(seq))
    tabs_s = tuple(jnp.tile(a, (nb, 1)) for a in _rope_tables(past_len + jnp.arange(t)))

    xp = x_prompt.reshape(mp, D_MODEL)
    xs = x_sample.reshape(ms, D_MODEL)
    mem = mem_prompt.reshape(batch * N_MEM, D_MODEL)
    fg = final_g.reshape(1, D_MODEL)
    vec = lambda a: a.reshape(1, -1)
    ck = cache_k.reshape(depth, n_phys, PAGE * N_HEADS, HEAD_W)
    cv = cache_v.reshape(depth, n_phys, PAGE * N_HEADS, HEAD_W)

    kv_p, kv_s = (), ()
    outs = {k: [] for k in ("cp", "mkp", "mvp", "cs")}
    for l in range(depth):
        lam_init = 0.8 - 0.6 * math.exp(-0.3 * l)
        final = l == depth - 1
        lamv = jnp.stack([lam_q1[l], lam_k1[l], lam_q2[l], lam_k2[l]])
        w_in_l = w_in[l].astype(BF16)
        wpc, wpa, wpx, wo = (w[l].astype(BF16) for w in (w_pc, w_pa, w_px, w_o))
        g_sub = vec(subln_g[l])
        conv_args = (w_dw[l], vec(b_dw[l]), vec(conv_ln_g[l]), vec(conv_ln_b[l]))

        mkv = _memkv(mem, vec(mem_norm_g[l]), jnp.concatenate([w_mk[l], w_mv[l]], axis=1).astype(BF16), tm=512)
        mk_p = mkv[0].reshape(batch, N_MEM, D_MODEL)
        mv_p = mkv[1].reshape(batch, N_MEM, D_MODEL)
        pp, kp_all, vp_all, vt = _proj(xp, vec(norm_g[l]), w_in_l, *tabs_p, kv_p, layer=l, depth=depth,
                                       tm=TM_PROJ, table_blocks=seq // TM_PROJ, out_dtype=BF16, tk=T_ATTN)
        kv_p = (kp_all, vp_all)
        on_p = _attn_prompt(pp, vt, lamv, g_sub, batch=batch, seq=seq, t=T_ATTN, hp=HEADS_PER_STEP,
                            lam_init=lam_init)
        cg_p = _conv_prompt(pp, *conv_args, seq=seq, tc=TC_CONV)
        xg_p = _xattn(pp, mk_p, mv_p, tx=TX_XATTN, out_dtype=BF16)
        xp = _merge(cg_p, on_p, xg_p, pp, xp, wpc, wpa, wpx, wo, fg, tm=TM_MERGE, final=final)

        ps, ks_all, vs_all = _proj(xs, vec(norm_g[l]), w_in_l, *tabs_s, kv_s, layer=l, depth=depth,
                                   tm=ms, table_blocks=1, out_dtype=F32, tk=0)
        kv_s = (ks_all, vs_all)
        on_s = _attn_paged(page_table, lamv, g_sub, ps, ks_all, vs_all, ck, cv, layer=l,
                           pages=PAGES_PER_STEP, lam_init=lam_init)
        u_s = ps[SLOT_U].reshape(nb, t, D_MODEL)
        ext_s = jnp.concatenate([cache_conv[l], u_s, jnp.zeros((nb, 2, D_MODEL), F32)], axis=1)
        cg_s = _conv_sample(ext_s, ps, *conv_args)
        xg_s = _xattn(ps, cache_mem_k[l].reshape(nb, N_MEM, D_MODEL), cache_mem_v[l].reshape(nb, N_MEM, D_MODEL),
                      tx=t, out_dtype=F32)
        xs = _merge(cg_s, on_s, xg_s, ps, xs, wpc, wpa, wpx, wo, fg, tm=ms, final=final)

        outs["cp"].append(pp[SLOT_U].reshape(batch, seq, D_MODEL)[:, seq - (CONV_K - 1):].astype(F32))
        outs["mkp"].append(mk_p.reshape(batch, N_MEM, X_HEADS, X_DIM))
        outs["mvp"].append(mv_p.reshape(batch, N_MEM, X_HEADS, X_DIM))
        outs["cs"].append(ext_s[:, t:t + CONV_K - 1])

    st = lambda k: jnp.stack(outs[k])
    heads_p = lambda a: a.reshape(depth, batch, seq, N_HEADS, HEAD_W)
    heads_s = lambda a: a.reshape(depth, nb, t, N_HEADS, HEAD_W)
    return (xp.reshape(batch, seq, D_MODEL), xs.reshape(nb, t, D_MODEL), heads_p(kv_p[0]), heads_p(kv_p[1]),
            st("cp"), st("mkp"), st("mvp"), heads_s(kv_s[0]), heads_s(kv_s[1]), st("cs"))
```

```python
import functools
import math

import jax
import jax.numpy as jnp
from jax import lax
from jax.experimental import pallas as pl
from jax.experimental.pallas import tpu as pltpu

D_MODEL = 1024
N_HEADS = 8
QK_DIM = 64
HEAD_W = 128
ROPE_DIM = 16
ROPE_THETA = 500000.0
CONV_K = 31
X_HEADS = 4
X_DIM = 256
N_MEM = 256
N_SPLIT = 12
PAGE = 128
RMS_EPS = 1e-6
LN_EPS = 1e-5
SUBLN_EPS = 1e-5
LANES = 128
SUBLANES = 8
CONV_HALO = 32
LOG2E = 1.4426950408889634

VMEM_LIMIT = 52 * 1024 * 1024

TM_PROJ = 1024
T_ATTN = 256
HEADS_PER_STEP = 4
TC_CONV = 256
TX_XATTN = 512
TM_MERGE = 256
PAGES_PER_STEP = 8
PAGE_SLOTS = 3

SLOT_U, SLOT_SZC, SLOT_Q, SLOT_K, SLOT_V, SLOT_SZA, SLOT_QX, SLOT_SZX, SLOT_SGC, SLOT_SGA, SLOT_SGX = range(11)
N_SLOTS = 11

BF16 = jnp.bfloat16
F32 = jnp.float32


def _params(sem):
    return pltpu.CompilerParams(dimension_semantics=sem, vmem_limit_bytes=VMEM_LIMIT)


def _sigmoid(x):
    return 0.5 * jnp.tanh(0.5 * x) + 0.5


def _silu(x):
    return x * _sigmoid(x)


def _rms_rows(x, eps):
    return x * lax.rsqrt(jnp.mean(x * x, axis=-1, keepdims=True) + eps)


def _dot(a, b):
    return jnp.dot(a, b, preferred_element_type=F32)


def _dot_nt(a, b):
    return lax.dot_general(a, b, (((1,), (1,)), ((), ())), preferred_element_type=F32)


def _diff_lambda(lam_ref, lam_init):
    lv = lam_ref[...]
    e1 = jnp.exp(jnp.sum(lv[0:1] * lv[1:2], axis=-1, keepdims=True))
    e2 = jnp.exp(jnp.sum(lv[2:3] * lv[3:4], axis=-1, keepdims=True))
    return e1 - e2 + lam_init


def _head_cols(h):
    return slice(h * HEAD_W, (h + 1) * HEAD_W)


def _head_rows(h, n):
    return pl.ds(h, n, stride=N_HEADS)


def _rope_cols(acc, cos, sin_lo, sin_hi):
    outs = []
    for c in range(D_MODEL // LANES):
        xc = acc[:, c * LANES:(c + 1) * LANES]
        outs.append(xc * cos + pltpu.roll(xc, LANES - 8, 1) * sin_lo + pltpu.roll(xc, 8, 1) * sin_hi)
    return jnp.concatenate(outs, axis=1)


def _proj_kernel(*refs, tm, tk, n_prev):
    x_ref, g_ref, w_ref, cos_ref, slo_ref, shi_ref = refs[:6]
    refs = refs[6 + n_prev:]
    p_ref, k_ref, v_ref = refs[:3]
    vt_ref = refs[3] if tk else None
    h_scr, a_scr = refs[-2:]
    j = pl.program_id(1)
    dt = p_ref.dtype

    @pl.when(j == 0)
    def _():
        h_scr[...] = (_rms_rows(x_ref[...], RMS_EPS) * g_ref[...]).astype(BF16)

    def acc():
        return _dot(h_scr[...], w_ref[...])

    def rope(x):
        return _rope_cols(x, cos_ref[...], slo_ref[...], shi_ref[...])

    @pl.when(j == 0)
    def _():
        a_scr[...] = acc()

    @pl.when(j == 1)
    def _():
        p_ref[...] = (a_scr[...] * _sigmoid(acc())).astype(dt)

    @pl.when((j == 2) | (j == 6) | (j == 8))
    def _():
        p_ref[...] = _silu(acc()).astype(dt)

    @pl.when(j == 3)
    def _():
        p_ref[...] = (rope(acc()) * (QK_DIM ** -0.5 * LOG2E)).astype(dt)

    @pl.when(j == 4)
    def _():
        k = rope(acc())
        p_ref[...] = k.astype(dt)
        for h in range(N_HEADS):
            k_ref[_head_rows(h, tm), :] = k[:, _head_cols(h)]

    @pl.when(j == 5)
    def _():
        v = acc()
        p_ref[...] = v.astype(dt)
        for h in range(N_HEADS):
            vh = v[:, _head_cols(h)]
            v_ref[_head_rows(h, tm), :] = vh
            if tk:
                for c in range(tm // tk):
                    vt_ref[h, c] = vh[c * tk:(c + 1) * tk, :].T.astype(BF16)

    @pl.when(j == 7)
    def _():
        p_ref[...] = (acc() * (X_DIM ** -0.5)).astype(dt)

    @pl.when(j >= 9)
    def _():
        p_ref[...] = _sigmoid(acc()).astype(dt)


def _proj(x, g, w, cos, slo, shi, kv_prev, *, layer, depth, tm, table_blocks, out_dtype, tk):
    m = x.shape[0]
    tab_spec = pl.BlockSpec((tm, LANES), lambda i, j: (i % table_blocks, 0))
    kv_spec = pl.BlockSpec((None, tm * N_HEADS, HEAD_W), lambda i, j: (layer, i, 0))
    kv_shape = jax.ShapeDtypeStruct((depth, m * N_HEADS, HEAD_W), F32)
    in_specs = [
        pl.BlockSpec((tm, D_MODEL), lambda i, j: (i, 0)),
        pl.BlockSpec((1, D_MODEL), lambda i, j: (0, 0)),
        pl.BlockSpec((D_MODEL, D_MODEL), lambda i, j: (0, j)),
        tab_spec, tab_spec, tab_spec,
    ] + [pl.BlockSpec(memory_space=pl.ANY)] * len(kv_prev)
    out_specs = [pl.BlockSpec((None, tm, D_MODEL), lambda i, j: (jnp.maximum(j - 1, 0), i, 0)), kv_spec, kv_spec]
    out_shape = [jax.ShapeDtypeStruct((N_SLOTS, m, D_MODEL), out_dtype), kv_shape, kv_shape]
    if tk:
        out_specs.append(pl.BlockSpec((N_HEADS, tm // tk, HEAD_W, tk), lambda i, j: (0, i, 0, 0)))
        out_shape.append(jax.ShapeDtypeStruct((N_HEADS, m // tk, HEAD_W, tk), BF16))
    return pl.pallas_call(
        functools.partial(_proj_kernel, tm=tm, tk=tk, n_prev=len(kv_prev)),
        grid=(m // tm, N_SPLIT),
        in_specs=in_specs,
        out_specs=out_specs,
        out_shape=out_shape,
        scratch_shapes=[pltpu.VMEM((tm, D_MODEL), BF16), pltpu.VMEM((tm, D_MODEL), F32)],
        input_output_aliases={6 + n: 1 + n for n in range(len(kv_prev))},
        compiler_params=_params(("parallel", "arbitrary")),
        name="proj",
    )(x, g, w, cos, slo, shi, *kv_prev)


def _memkv_kernel(x_ref, g_ref, w_ref, o_ref, h_scr):
    @pl.when(pl.program_id(1) == 0)
    def _():
        h_scr[...] = (_rms_rows(x_ref[...], RMS_EPS) * g_ref[...]).astype(BF16)

    o_ref[...] = _dot(h_scr[...], w_ref[...])


def _memkv(mem, g, w, *, tm):
    m = mem.shape[0]
    n_groups = w.shape[1] // D_MODEL
    return pl.pallas_call(
        _memkv_kernel,
        grid=(m // tm, n_groups),
        in_specs=[
            pl.BlockSpec((tm, D_MODEL), lambda i, j: (i, 0)),
            pl.BlockSpec((1, D_MODEL), lambda i, j: (0, 0)),
            pl.BlockSpec((D_MODEL, D_MODEL), lambda i, j: (0, j)),
        ],
        out_specs=pl.BlockSpec((None, tm, D_MODEL), lambda i, j: (j, i, 0)),
        out_shape=jax.ShapeDtypeStruct((n_groups, m, D_MODEL), F32),
        scratch_shapes=[pltpu.VMEM((tm, D_MODEL), BF16)],
        compiler_params=_params(("parallel", "arbitrary")),
        name="memkv",
    )(mem, g, w)


def _split_maps(q):
    lane = lax.broadcasted_iota(jnp.int32, q.shape, 1)
    zero = jnp.zeros_like(q)
    return jnp.concatenate([jnp.where(lane < QK_DIM, q, zero), jnp.where(lane >= QK_DIM, q, zero)], axis=0)


def _attn_kernel(lam_ref, g_ref, q_ref, k_ref, vt_ref, o_ref, sa_scr, sb_scr, acc_scr, *, t, hp, lam_init):
    qi = pl.program_id(2)
    qs = [_split_maps(q_ref[:, _head_cols(h)]) for h in range(hp)]

    def scores(h, j):
        off = pl.multiple_of(j * t, t)
        return _dot_nt(k_ref[pl.ds(off, t), _head_cols(h)], qs[h])

    def update(h, j, st, stats):
        m, l = stats
        m_new = jnp.maximum(m, jnp.max(st, axis=0, keepdims=True))
        alpha = jnp.exp2(m - m_new)
        p = jnp.exp2(st - m_new)
        l_new = alpha * l + jnp.sum(p, axis=0, keepdims=True)
        acc_scr[h] = alpha * acc_scr[h] + _dot(vt_ref[h, j], p.astype(BF16))
        return m_new, l_new

    def full_step(j, src, dst, carry):
        out = []
        for h in range(hp):
            st = src[h]
            dst[h] = scores(h, j + 1)
            out.append(update(h, j, st, carry[h]))
        return tuple(out)

    acc_scr[...] = jnp.zeros(acc_scr.shape, F32)
    for h in range(hp):
        sa_scr[h] = scores(h, 0)
    init = tuple((jnp.full((1, 2 * t), -jnp.inf, F32), jnp.zeros((1, 2 * t), F32)) for _ in range(hp))

    def pair(jj, carry):
        carry = full_step(2 * jj, sa_scr, sb_scr, carry)
        return full_step(2 * jj + 1, sb_scr, sa_scr, carry)

    carry = lax.fori_loop(0, qi // 2, pair, init)
    odd = (qi % 2) == 1
    carry = lax.cond(odd, lambda c: full_step(qi - 1, sa_scr, sb_scr, c), lambda c: c, carry)

    lam = _diff_lambda(lam_ref, lam_init)
    key = lax.broadcasted_iota(jnp.int32, (t, 2 * t), 0)
    qry = lax.broadcasted_iota(jnp.int32, (t, 2 * t), 1) % t
    for h in range(hp):
        st = jnp.where(odd, sb_scr[h], sa_scr[h])
        _, l = update(h, qi, jnp.where(key <= qry, st, -jnp.inf), carry[h])
        acc = acc_scr[h]
        o = acc[:, :t] / l[:, :t] - lam * (acc[:, t:] / l[:, t:])
        o = o * lax.rsqrt(jnp.mean(o * o, axis=0, keepdims=True) + SUBLN_EPS)
        o_ref[:, _head_cols(h)] = (o.T * g_ref[...] * (1.0 - lam_init)).astype(o_ref.dtype)


def _attn_prompt(p, vt, lamv, g, *, batch, seq, t, hp, lam_init):
    m = batch * seq
    nq = seq // t
    w = hp * HEAD_W
    return pl.pallas_call(
        functools.partial(_attn_kernel, t=t, hp=hp, lam_init=lam_init),
        grid=(batch, N_HEADS // hp, nq),
        in_specs=[
            pl.BlockSpec((4, QK_DIM), lambda b, h, i: (0, 0)),
            pl.BlockSpec((1, HEAD_W), lambda b, h, i: (0, 0)),
            pl.BlockSpec((None, t, w), lambda b, h, i: (SLOT_Q, b * nq + i, h)),
            pl.BlockSpec((None, seq, w), lambda b, h, i: (SLOT_K, b, h)),
            pl.BlockSpec((hp, nq, HEAD_W, t), lambda b, h, i: (h, b, 0, 0)),
        ],
        out_specs=pl.BlockSpec((t, w), lambda b, h, i: (b * nq + i, h)),
        out_shape=jax.ShapeDtypeStruct((m, D_MODEL), BF16),
        scratch_shapes=[
            pltpu.VMEM((hp, t, 2 * t), F32),
            pltpu.VMEM((hp, t, 2 * t), F32),
            pltpu.VMEM((hp, HEAD_W, 2 * t), F32),
        ],
        compiler_params=_params(("parallel", "parallel", "arbitrary")),
        name="attn_prompt",
    )(lamv, g, p, p, vt)


def _paged_kernel(pt_ref, lam_ref, g_ref, q_ref, kn_ref, vn_ref, ck_hbm, cv_hbm, o_ref,
                  kbuf, vbuf, sem, qbd_scr, m_scr, l_scr, acc_scr, *, layer, pages, steps, t, lam_init):
    step = pl.program_id(1)
    g = pl.program_id(0) * steps + step
    total = pl.num_programs(0) * steps

    def page_copies(gi):
        bi, si, slot = gi // steps, gi % steps, gi % PAGE_SLOTS
        copies = []
        for i in range(pages):
            page = pt_ref[bi, si * pages + i]
            copies.append(pltpu.make_async_copy(ck_hbm.at[layer, page], kbuf.at[slot, i], sem.at[0, slot]))
            copies.append(pltpu.make_async_copy(cv_hbm.at[layer, page], vbuf.at[slot, i], sem.at[1, slot]))
        return copies

    @pl.when(g == 0)
    def _():
        for gi in range(PAGE_SLOTS - 1):
            for c in page_copies(gi):
                c.start()

    @pl.when(g + PAGE_SLOTS - 1 < total)
    def _():
        for c in page_copies(g + PAGE_SLOTS - 1):
            c.start()

    for c in page_copies(g):
        c.wait()
    slot = g % PAGE_SLOTS
    k_refs = [kbuf.at[slot, i] for i in range(pages)]
    v_refs = [vbuf.at[slot, i] for i in range(pages)]

    @pl.when(step == 0)
    def _():
        for h in range(N_HEADS):
            qbd_scr[h] = _split_maps(q_ref[:, _head_cols(h)]).astype(BF16)
        m_scr[...] = jnp.full(m_scr.shape, -jnp.inf, F32)
        l_scr[...] = jnp.zeros(l_scr.shape, F32)
        acc_scr[...] = jnp.zeros(acc_scr.shape, F32)

    s = jnp.concatenate(
        [jnp.concatenate([_dot_nt(qbd_scr[h], k_refs[i][_head_rows(h, PAGE), :].astype(BF16))
                          for i in range(pages)], axis=1) for h in range(N_HEADS)], axis=0)
    m_old = m_scr[...]
    m_new = jnp.maximum(m_old, jnp.max(s, axis=-1, keepdims=True))
    alpha = jnp.exp2(m_old - m_new)
    p = jnp.exp2(s - m_new[:, :1])
    l_scr[...] = alpha * l_scr[...] + jnp.sum(p, axis=-1, keepdims=True)
    m_scr[...] = m_new
    p = p.astype(BF16)
    pvs = []
    for h in range(N_HEADS):
        ph = p[h * 2 * t:(h + 1) * 2 * t]
        pv = _dot(ph[:, :PAGE], v_refs[0][_head_rows(h, PAGE), :].astype(BF16))
        for i in range(1, pages):
            pv = pv + _dot(ph[:, i * PAGE:(i + 1) * PAGE], v_refs[i][_head_rows(h, PAGE), :].astype(BF16))
        pvs.append(pv)
    acc_scr[...] = alpha * acc_scr[...] + jnp.concatenate(pvs, axis=0)

    @pl.when(step == pl.num_programs(1) - 1)
    def _():
        lam = _diff_lambda(lam_ref, lam_init)
        for h in range(N_HEADS):
            rows = _head_rows(h, t)
            qh = qbd_scr[h].astype(F32)
            s = _dot_nt(qh, kn_ref[rows, :])
            row = lax.broadcasted_iota(jnp.int32, s.shape, 0) % t
            col = lax.broadcasted_iota(jnp.int32, s.shape, 1)
            s = jnp.where(col <= row, s, -jnp.inf)
            hr = slice(h * 2 * t, (h + 1) * 2 * t)
            m_old = m_scr[hr, :1]
            m_new = jnp.maximum(m_old, jnp.max(s, axis=-1, keepdims=True))
            alpha = jnp.exp2(m_old - m_new)
            p = jnp.exp2(s - m_new)
            l = alpha * l_scr[hr, :1] + jnp.sum(p, axis=-1, keepdims=True)
            acc = alpha * acc_scr[hr, :] + _dot(p, vn_ref[rows, :])
            o = acc[:t] / l[:t] - lam * (acc[t:] / l[t:])
            o_ref[:, _head_cols(h)] = _rms_rows(o, SUBLN_EPS) * g_ref[...] * (1.0 - lam_init)


def _attn_paged(page_table, lamv, g, ps, k_all, v_all, cache_k, cache_v, *, layer, pages, lam_init):
    nb, n_pages = page_table.shape
    t = ps.shape[1] // nb
    steps = n_pages // pages
    assert nb * steps >= PAGE_SLOTS - 1
    page_buf = pltpu.VMEM((PAGE_SLOTS, pages, PAGE * N_HEADS, HEAD_W), F32)
    new_spec = pl.BlockSpec((None, t * N_HEADS, HEAD_W), lambda b, s, pt: (layer, b, 0))
    grid_spec = pltpu.PrefetchScalarGridSpec(
        num_scalar_prefetch=1,
        grid=(nb, steps),
        in_specs=[
            pl.BlockSpec((4, QK_DIM), lambda b, s, pt: (0, 0)),
            pl.BlockSpec((1, HEAD_W), lambda b, s, pt: (0, 0)),
            pl.BlockSpec((None, t, D_MODEL), lambda b, s, pt: (SLOT_Q, b, 0)),
            new_spec, new_spec,
            pl.BlockSpec(memory_space=pl.ANY), pl.BlockSpec(memory_space=pl.ANY),
        ],
        out_specs=pl.BlockSpec((t, D_MODEL), lambda b, s, pt: (b, 0)),
        scratch_shapes=[
            page_buf, page_buf, pltpu.SemaphoreType.DMA((2, PAGE_SLOTS)),
            pltpu.VMEM((N_HEADS, 2 * t, HEAD_W), BF16),
            pltpu.VMEM((N_HEADS * 2 * t, LANES), F32),
            pltpu.VMEM((N_HEADS * 2 * t, LANES), F32),
            pltpu.VMEM((N_HEADS * 2 * t, HEAD_W), F32),
        ],
    )
    return pl.pallas_call(
        functools.partial(_paged_kernel, layer=layer, pages=pages, steps=steps, t=t, lam_init=lam_init),
        grid_spec=grid_spec,
        out_shape=jax.ShapeDtypeStruct((nb * t, D_MODEL), F32),
        compiler_params=_params(("arbitrary", "arbitrary")),
        name="attn_paged",
    )(page_table, lamv, g, ps, k_all, v_all, cache_k, cache_v)


def _ln_swish_gate(c, ln_g, ln_b, szc):
    mu = jnp.mean(c, axis=-1, keepdims=True)
    xc = c - mu
    var = jnp.mean(xc * xc, axis=-1, keepdims=True)
    y = xc * lax.rsqrt(var + LN_EPS) * ln_g + ln_b
    return _silu(y) * szc


def _conv_prompt_kernel(u_ref, prev_ref, szc_ref, w_ref, b_ref, lg_ref, lb_ref, o_ref, ext_scr, c_scr,
                        *, tc, tiles_per_seq, rb):
    first = (pl.program_id(0) % tiles_per_seq) == 0
    keep = jnp.where(first, 0.0, 1.0)
    ext_scr[0:CONV_HALO, :] = prev_ref[...].astype(F32) * keep
    ext_scr[CONV_HALO:CONV_HALO + tc, :] = u_ref[...].astype(F32)
    ext_scr[CONV_HALO + tc:, :] = jnp.zeros((SUBLANES, D_MODEL), F32)
    shift = CONV_HALO - (CONV_K - 1)
    for r in range(tc // rb):
        for c in range(D_MODEL // LANES):
            cols = slice(c * LANES, (c + 1) * LANES)
            acc = jnp.zeros((rb, LANES), F32)
            for b in range(SUBLANES):
                z = jnp.zeros((rb + SUBLANES, LANES), F32)
                for a in range((shift + CONV_K - 1) // SUBLANES + 1):
                    j = a * SUBLANES + b - shift
                    if 0 <= j < CONV_K:
                        lo = r * rb + a * SUBLANES
                        z = z + ext_scr[lo:lo + rb + SUBLANES, cols] * w_ref[j:j + 1, cols]
                acc = acc + z[b:b + rb]
            c_scr[r * rb:(r + 1) * rb, cols] = acc + b_ref[:, cols]
    o_ref[...] = _ln_swish_gate(c_scr[...], lg_ref[...], lb_ref[...], szc_ref[...].astype(F32)).astype(o_ref.dtype)


def _conv_prompt(p, w_dw, b_dw, ln_g, ln_b, *, seq, tc):
    m = p.shape[1]
    halo_blocks = tc // CONV_HALO
    vec = pl.BlockSpec((1, D_MODEL), lambda i: (0, 0))
    return pl.pallas_call(
        functools.partial(_conv_prompt_kernel, tc=tc, tiles_per_seq=seq // tc, rb=64),
        grid=(m // tc,),
        in_specs=[
            pl.BlockSpec((None, tc, D_MODEL), lambda i: (SLOT_U, i, 0)),
            pl.BlockSpec((None, CONV_HALO, D_MODEL), lambda i: (SLOT_U, jnp.maximum(i * halo_blocks - 1, 0), 0)),
            pl.BlockSpec((None, tc, D_MODEL), lambda i: (SLOT_SZC, i, 0)),
            pl.BlockSpec((CONV_K, D_MODEL), lambda i: (0, 0)),
            vec, vec, vec,
        ],
        out_specs=pl.BlockSpec((tc, D_MODEL), lambda i: (i, 0)),
        out_shape=jax.ShapeDtypeStruct((m, D_MODEL), BF16),
        scratch_shapes=[pltpu.VMEM((tc + CONV_HALO + SUBLANES, D_MODEL), F32), pltpu.VMEM((tc, D_MODEL), F32)],
        compiler_params=_params(("parallel",)),
        name="conv_prompt",
    )(p, p, p, w_dw, b_dw, ln_g, ln_b)


def _conv_sample_kernel(ext_ref, szc_ref, w_ref, b_ref, lg_ref, lb_ref, o_ref, c_scr, *, nb, t):
    def seq_body(b, _):
        acc = jnp.zeros((t, D_MODEL), F32)
        for j in range(CONV_K):
            acc = acc + ext_ref[b, j:j + t, :] * w_ref[j:j + 1, :]
        c_scr[pl.ds(pl.multiple_of(b * t, t), t), :] = acc + b_ref[...]
        return 0

    lax.fori_loop(0, nb, seq_body, 0)
    o_ref[...] = _ln_swish_gate(c_scr[...], lg_ref[...], lb_ref[...], szc_ref[...]).astype(o_ref.dtype)


def _conv_sample(ext, ps, w_dw, b_dw, ln_g, ln_b):
    nb, rows, _ = ext.shape
    m = ps.shape[1]
    t = m // nb
    vec = pl.BlockSpec((1, D_MODEL), lambda i: (0, 0))
    return pl.pallas_call(
        functools.partial(_conv_sample_kernel, nb=nb, t=t),
        grid=(1,),
        in_specs=[
            pl.BlockSpec((nb, rows, D_MODEL), lambda i: (0, 0, 0)),
            pl.BlockSpec((None, m, D_MODEL), lambda i: (SLOT_SZC, 0, 0)),
            pl.BlockSpec((CONV_K, D_MODEL), lambda i: (0, 0)),
            vec, vec, vec,
        ],
        out_specs=pl.BlockSpec((m, D_MODEL), lambda i: (0, 0)),
        out_shape=jax.ShapeDtypeStruct((m, D_MODEL), F32),
        scratch_shapes=[pltpu.VMEM((m, D_MODEL), F32)],
        compiler_params=_params(("arbitrary",)),
        name="conv_sample",
    )(ext, ps, w_dw, b_dw, ln_g, ln_b)


def _xattn_kernel(q_ref, szx_ref, mk_ref, mv_ref, o_ref):
    outs = []
    for h in range(X_HEADS):
        cols = slice(h * X_DIM, (h + 1) * X_DIM)
        q = q_ref[:, cols].astype(BF16)
        s = _dot_nt(q, mk_ref[:, cols].astype(BF16))
        p = jnp.exp(s - jnp.max(s, axis=-1, keepdims=True))
        o = _dot(p.astype(BF16), mv_ref[:, cols].astype(BF16))
        outs.append(o / jnp.sum(p, axis=-1, keepdims=True))
    o_ref[...] = (jnp.concatenate(outs, axis=1) * szx_ref[...].astype(F32)).astype(o_ref.dtype)


def _xattn(p, mem_k, mem_v, *, tx, out_dtype):
    m = p.shape[1]
    groups = mem_k.shape[0]
    tiles = m // groups // tx
    mem_spec = pl.BlockSpec((None, N_MEM, D_MODEL), lambda g, i: (g, 0, 0))
    return pl.pallas_call(
        _xattn_kernel,
        grid=(groups, tiles),
        in_specs=[
            pl.BlockSpec((None, tx, D_MODEL), lambda g, i: (SLOT_QX, g * tiles + i, 0)),
            pl.BlockSpec((None, tx, D_MODEL), lambda g, i: (SLOT_SZX, g * tiles + i, 0)),
            mem_spec, mem_spec,
        ],
        out_specs=pl.BlockSpec((tx, D_MODEL), lambda g, i: (g * tiles + i, 0)),
        out_shape=jax.ShapeDtypeStruct((m, D_MODEL), out_dtype),
        compiler_params=_params(("parallel", "arbitrary")),
        name="xattn",
    )(p, p, mem_k, mem_v)


def _merge_kernel(cg_ref, on_ref, sza_ref, xg_ref, sgc_ref, sga_ref, sgx_ref, x_ref,
                  wpc_ref, wpa_ref, wpx_ref, wo_ref, fg_ref, o_ref, *, final):
    f = lambda r: r[...].astype(F32)
    yc = _dot(cg_ref[...].astype(BF16), wpc_ref[...])
    ya = _dot((f(on_ref) * f(sza_ref)).astype(BF16), wpa_ref[...])
    yx = _dot(xg_ref[...].astype(BF16), wpx_ref[...])
    merged = f(sgc_ref) * yc + f(sga_ref) * ya + f(sgx_ref) * yx
    x_new = x_ref[...] + _dot(merged.astype(BF16), wo_ref[...])
    if final:
        x_new = _rms_rows(x_new, RMS_EPS) * fg_ref[...]
    o_ref[...] = x_new


def _merge(cg, on, xg, p, x, w_pc, w_pa, w_px, w_o, final_g, *, tm, final):
    m = x.shape[0]
    row = pl.BlockSpec((tm, D_MODEL), lambda i: (i, 0))
    slot = lambda s: pl.BlockSpec((None, tm, D_MODEL), lambda i: (s, i, 0))
    wspec = pl.BlockSpec((D_MODEL, D_MODEL), lambda i: (0, 0))
    return pl.pallas_call(
        functools.partial(_merge_kernel, final=final),
        grid=(m // tm,),
        in_specs=[row, row, slot(SLOT_SZA), row, slot(SLOT_SGC), slot(SLOT_SGA), slot(SLOT_SGX), row,
                  wspec, wspec, wspec, wspec, pl.BlockSpec((1, D_MODEL), lambda i: (0, 0))],
        out_specs=row,
        out_shape=jax.ShapeDtypeStruct((m, D_MODEL), F32),
        compiler_params=_params(("parallel",)),
        name="merge",
    )(cg, on, p, xg, p, p, p, x, w_pc, w_pa, w_px, w_o, final_g)


def _rope_tables(pos):
    half = ROPE_DIM // 2
    inv = ROPE_THETA ** (-jnp.arange(half, dtype=F32) * 2.0 / ROPE_DIM)
    ang = pos.astype(F32)[:, None] * inv[None, :]
    cos, sin = jnp.cos(ang), jnp.sin(ang)
    n = pos.shape[0]
    pad = jnp.zeros((n, QK_DIM - ROPE_DIM), F32)
    zero = jnp.zeros((n, half), F32)
    cos64 = jnp.concatenate([cos, cos, pad + 1.0], axis=1)
    lo64 = jnp.concatenate([-sin, zero, pad], axis=1)
    hi64 = jnp.concatenate([zero, sin, pad], axis=1)
    rep = lambda a: jnp.tile(a, (1, LANES // QK_DIM))
    return rep(cos64), rep(lo64), rep(hi64)


def kernel(x_prompt, x_sample, cache_k, cache_v, cache_conv, cache_mem_k, cache_mem_v, page_table, mem_prompt,
           norm_g, w_in, w_dw, b_dw, conv_ln_g, conv_ln_b, w_pc, lam_q1, lam_k1, lam_q2, lam_k2, subln_g,
           w_pa, mem_norm_g, w_mk, w_mv, w_px, w_o, final_g):
    batch, seq, _ = x_prompt.shape
    nb, t, _ = x_sample.shape
    depth = w_in.shape[0]
    n_phys = cache_k.shape[1]
    past_len = page_table.shape[1] * PAGE
    mp, ms = batch * seq, nb * t

    tabs_p = _rope_tables(jnp.arange(seq))
    tabs_s = tuple(jnp.tile(a, (nb, 1)) for a in _rope_tables(past_len + jnp.arange(t)))

    xp = x_prompt.reshape(mp, D_MODEL)
    xs = x_sample.reshape(ms, D_MODEL)
    mem = mem_prompt.reshape(batch * N_MEM, D_MODEL)
    fg = final_g.reshape(1, D_MODEL)
    vec = lambda a: a.reshape(1, -1)
    ck = cache_k.reshape(depth, n_phys, PAGE * N_HEADS, HEAD_W)
    cv = cache_v.reshape(depth, n_phys, PAGE * N_HEADS, HEAD_W)

    kv_p = tuple(jnp.zeros((depth, mp * N_HEADS, HEAD_W), F32) for _ in range(2))
    kv_s = tuple(jnp.zeros((depth, ms * N_HEADS, HEAD_W), F32) for _ in range(2))
    outs = {k: [] for k in ("cp", "mkp", "mvp", "cs")}
    for l in range(depth):
        lam_init = 0.8 - 0.6 * math.exp(-0.3 * l)
        final = l == depth - 1
        lamv = jnp.stack([lam_q1[l], lam_k1[l], lam_q2[l], lam_k2[l]])
        w_in_l = w_in[l].astype(BF16)
        wpc, wpa, wpx, wo = (w[l].astype(BF16) for w in (w_pc, w_pa, w_px, w_o))
        g_sub = vec(subln_g[l])
        conv_args = (w_dw[l], vec(b_dw[l]), vec(conv_ln_g[l]), vec(conv_ln_b[l]))

        mkv = _memkv(mem, vec(mem_norm_g[l]), jnp.concatenate([w_mk[l], w_mv[l]], axis=1).astype(BF16), tm=512)
        mk_p = mkv[0].reshape(batch, N_MEM, D_MODEL)
        mv_p = mkv[1].reshape(batch, N_MEM, D_MODEL)
        pp, kp_all, vp_all, vt = _proj(xp, vec(norm_g[l]), w_in_l, *tabs_p, kv_p, layer=l, depth=depth,
                                       tm=TM_PROJ, table_blocks=seq // TM_PROJ, out_dtype=BF16, tk=T_ATTN)
        kv_p = (kp_all, vp_all)
        on_p = _attn_prompt(pp, vt, lamv, g_sub, batch=batch, seq=seq, t=T_ATTN, hp=HEADS_PER_STEP,
                            lam_init=lam_init)
        cg_p = _conv_prompt(pp, *conv_args, seq=seq, tc=TC_CONV)
        xg_p = _xattn(pp, mk_p, mv_p, tx=TX_XATTN, out_dtype=BF16)
        xp = _merge(cg_p, on_p, xg_p, pp, xp, wpc, wpa, wpx, wo, fg, tm=TM_MERGE, final=final)

        ps, ks_all, vs_all = _proj(xs, vec(norm_g[l]), w_in_l, *tabs_s, kv_s, layer=l, depth=depth,
                                   tm=ms, table_blocks=1, out_dtype=F32, tk=0)
        kv_s = (ks_all, vs_all)
        on_s = _attn_paged(page_table, lamv, g_sub, ps, ks_all, vs_all, ck, cv, layer=l,
                           pages=PAGES_PER_STEP, lam_init=lam_init)
        u_s = ps[SLOT_U].reshape(nb, t, D_MODEL)
        ext_s = jnp.concatenate([cache_conv[l], u_s, jnp.zeros((nb, 2, D_MODEL), F32)], axis=1)
        cg_s = _conv_sample(ext_s, ps, *conv_args)
        xg_s = _xattn(ps, cache_mem_k[l].reshape(nb, N_MEM, D_MODEL), cache_mem_v[l].reshape(nb, N_MEM, D_MODEL),
                      tx=t, out_dtype=F32)
        xs = _merge(cg_s, on_s, xg_s, ps, xs, wpc, wpa, wpx, wo, fg, tm=ms, final=final)

        outs["cp"].append(pp[SLOT_U].reshape(batch, seq, D_MODEL)[:, seq - (CONV_K - 1):].astype(F32))
        outs["mkp"].append(mk_p.reshape(batch, N_MEM, X_HEADS, X_DIM))
        outs["mvp"].append(mv_p.reshape(batch, N_MEM, X_HEADS, X_DIM))
        outs["cs"].append(ext_s[:, t:t + CONV_K - 1])

    st = lambda k: jnp.stack(outs[k])
    heads_p = lambda a: a.reshape(depth, batch, seq, N_HEADS, HEAD_W)
    heads_s = lambda a: a.reshape(depth, nb, t, N_HEADS, HEAD_W)
    return (xp.reshape(batch, seq, D_MODEL), xs.reshape(nb, t, D_MODEL), heads_p(kv_p[0]), heads_p(kv_p[1]),
            st("cp"), st("mkp"), st("mvp"), heads_s(kv_s[0]), heads_s(kv_s[1]), st("cs"))
```
